```python
import math
import jax, jax.numpy as jnp
from jax import lax
import numpy as np

D_MODEL = 1024
BATCH = 16
SEQ = 2048
DEPTH = 4

HEAD_DIM = 64
ROT_DIM = HEAD_DIM // 4
ROPE_THETA = 500000.0

A_Q_HEADS = 8
A_KV_HEADS = 2
A_WINDOW = 128

B_GROUPS = ((128, 1), (512, 4), (2048, 16))
N_B_GROUPS = 3
B_HEADS_PER_GROUP = 4

D_FF = 4 * D_MODEL
PLE_DIM = 256
N_BRANCHES = 2
LN_EPS = 1e-5
DEEPNORM_ALPHA = (2 * DEPTH) ** 0.25
DEEPNORM_BETA = (8 * DEPTH) ** -0.25

A_Q_W = A_Q_HEADS * HEAD_DIM
A_KV_W = A_KV_HEADS * HEAD_DIM
B_HEADS = N_B_GROUPS * B_HEADS_PER_GROUP
B_W = B_HEADS * HEAD_DIM
B_OUT_W = B_HEADS_PER_GROUP * HEAD_DIM
GATE_W = N_BRANCHES * D_MODEL
D_IN = A_Q_W + 2 * A_KV_W + 3 * B_W + GATE_W
IN_SPLITS = (A_Q_W, A_Q_W + A_KV_W, A_Q_W + 2 * A_KV_W,
             A_Q_W + 2 * A_KV_W + B_W, A_Q_W + 2 * A_KV_W + 2 * B_W,
             A_Q_W + 2 * A_KV_W + 3 * B_W)

kernel_name = "hybrid_gated_window_dilated_encoder"


def _layer_norm(x, g, b):
    xf = x.astype(jnp.float32)
    mu = jnp.mean(xf, axis=-1, keepdims=True)
    var = jnp.mean(jnp.square(xf - mu), axis=-1, keepdims=True)
    y = (xf - mu) * lax.rsqrt(var + LN_EPS) * g.astype(jnp.float32) + b.astype(jnp.float32)
    return y.astype(x.dtype)


def _rope_tables(seq):
    pos = jnp.arange(seq, dtype=jnp.float32)
    inv_freq = ROPE_THETA ** (-jnp.arange(0, ROT_DIM, 2, dtype=jnp.float32) / ROT_DIM)
    ang = pos[:, None] * inv_freq[None, :]
    return jnp.cos(ang), jnp.sin(ang)


def _rope(t, cos, sin):
    half = ROT_DIM // 2
    tf = t.astype(jnp.float32)
    t1, t2 = tf[..., :half], tf[..., half:ROT_DIM]
    c, s = cos[:, None, :], sin[:, None, :]
    out = jnp.concatenate([t1 * c - t2 * s, t2 * c + t1 * s, tf[..., ROT_DIM:]], axis=-1)
    return out.astype(t.dtype)


def _banded_attention(q, k, v, window, sink=None):
    n, seq, hq, dh = q.shape
    hkv = k.shape[2]
    grp = hq // hkv
    bs = window
    nblk = -(-seq // bs)
    lp = nblk * bs
    scale = 1.0 / math.sqrt(dh)
    qp = jnp.pad(q, ((0, 0), (0, lp - seq), (0, 0), (0, 0)))
    kp = jnp.pad(k, ((0, 0), (bs, lp - seq + bs), (0, 0), (0, 0))).astype(jnp.float32)
    vp = jnp.pad(v, ((0, 0), (bs, lp - seq + bs), (0, 0), (0, 0))).astype(jnp.float32)
    qb = qp.reshape(n, nblk, bs, hkv, grp, dh).transpose(1, 0, 2, 3, 4, 5)
    sink_b = None if sink is None else sink.astype(jnp.float32).reshape(1, hkv, grp, 1)

    def block(args):
        j, qj = args
        kj = lax.dynamic_slice_in_dim(kp, j * bs, 3 * bs, axis=1)
        vj = lax.dynamic_slice_in_dim(vp, j * bs, 3 * bs, axis=1)
        s = jnp.einsum('nqkgd,nskd->nkgqs', qj.astype(jnp.float32) * scale, kj)
        qpos = j * bs + jnp.arange(bs)
        kpos = j * bs - bs + jnp.arange(3 * bs)
        mask = (jnp.abs(qpos[:, None] - kpos[None, :]) <= window) & (kpos >= 0)[None, :] & (kpos < seq)[None, :]
        s = jnp.where(mask, s, -jnp.inf)
        m = jnp.max(s, axis=-1)
        if sink_b is not None:
            m = jnp.maximum(m, sink_b)
        e = jnp.exp(s - m[..., None])
        denom = jnp.sum(e, axis=-1)
        if sink_b is not None:
            denom = denom + jnp.exp(sink_b - m)
        o = jnp.einsum('nkgqs,nskd->nqkgd', e, vj) / denom.transpose(0, 3, 1, 2)[..., None]
        lse = (m + jnp.log(denom)).transpose(0, 3, 1, 2)
        return o, lse

    o, lse = lax.map(block, (jnp.arange(nblk), qb))
    o = o.transpose(1, 0, 2, 3, 4, 5).reshape(n, lp, hq, dh)[:, :seq].astype(q.dtype)
    lse = lse.transpose(1, 0, 2, 3, 4).reshape(n, lp, hq)[:, :seq]
    return o, lse


def _dilated_attention(q, k, v, window, dilation):
    n, seq, h, dh = q.shape
    r = dilation

    def sub(t):
        return t.reshape(n, seq // r, r, h, dh).transpose(0, 2, 1, 3, 4).reshape(n * r, seq // r, h, dh)

    o, lse = _banded_attention(sub(q), sub(k), sub(v), (window // 2) // r)
    o = o.reshape(n, r, seq // r, h, dh).transpose(0, 2, 1, 3, 4).reshape(n, seq, h, dh)
    lse = lse.reshape(n, r, seq // r, h).transpose(0, 2, 1, 3).reshape(n, seq, h)
    return o, lse


def _mixer_sublayer(x, cos, sin, w_in, b_gate, a_sink, w_branch_a, w_branch_b, w_out):
    bsz, seq, _ = x.shape
    proj = jnp.einsum('bsd,de->bse', x, w_in)
    qa, ka, va, qb, kb, vb, gl = jnp.split(proj, IN_SPLITS, axis=-1)

    qa = _rope(qa.reshape(bsz, seq, A_Q_HEADS, HEAD_DIM), cos, sin)
    ka = _rope(ka.reshape(bsz, seq, A_KV_HEADS, HEAD_DIM), cos, sin)
    va = va.reshape(bsz, seq, A_KV_HEADS, HEAD_DIM)
    oa, _ = _banded_attention(qa, ka, va, A_WINDOW, a_sink)
    ya = jnp.einsum('bse,ed->bsd', oa.reshape(bsz, seq, A_Q_W), w_branch_a)

    qb = _rope(qb.reshape(bsz, seq, B_HEADS, HEAD_DIM), cos, sin)
    kb = _rope(kb.reshape(bsz, seq, B_HEADS, HEAD_DIM), cos, sin)
    vb = vb.reshape(bsz, seq, B_HEADS, HEAD_DIM)
    outs, lses = [], []
    for g, (window, dil) in enumerate(B_GROUPS):
        hs = slice(g * B_HEADS_PER_GROUP, (g + 1) * B_HEADS_PER_GROUP)
        o, lse = _dilated_attention(qb[:, :, hs], kb[:, :, hs], vb[:, :, hs], window, dil)
        outs.append(o)
        lses.append(lse)
    wts = jax.nn.softmax(jnp.stack(lses), axis=0)
    ob = jnp.einsum('gbsh,gbshd->bshd', wts, jnp.stack(outs).astype(jnp.float32)).astype(x.dtype)
    yb = jnp.einsum('bse,ed->bsd', ob.reshape(bsz, seq, B_OUT_W), w_branch_b)

    gates = jax.nn.sigmoid((gl + b_gate).astype(jnp.float32)).astype(x.dtype)
    ga, gb = jnp.split(gates, N_BRANCHES, axis=-1)
    return jnp.einsum('bsd,de->bse', ga * ya + gb * yb, w_out)


def setup_inputs(seed: int = 0) -> dict:
    key = jax.random.key(seed)
    ks = jax.random.split(key, 24)
    L, D = DEPTH, D_MODEL
    beta = DEEPNORM_BETA

    def dense(k, shape, fan_in, scale=1.0):
        return jax.random.normal(k, shape, jnp.float32) * (scale * fan_in ** -0.5)

    x = jax.random.normal(ks[0], (BATCH, SEQ, D), jnp.float32)
    p = jax.random.normal(ks[1], (DEPTH, BATCH, SEQ, PLE_DIM), jnp.float32)
    w_in = jnp.concatenate([
        dense(ks[2], (L, D, A_Q_W), D),
        dense(ks[3], (L, D, A_KV_W), D),
        dense(ks[4], (L, D, A_KV_W), D, beta),
        dense(ks[5], (L, D, B_W), D),
        dense(ks[6], (L, D, B_W), D),
        dense(ks[7], (L, D, B_W), D, beta),
        dense(ks[8], (L, D, GATE_W), D),
    ], axis=-1)
    b_gate = 0.1 * jax.random.normal(ks[9], (L, GATE_W), jnp.float32)
    a_sink = 0.5 * jax.random.normal(ks[10], (L, A_Q_HEADS), jnp.float32)
    w_branch_a = dense(ks[11], (L, A_Q_W, D), A_Q_W, beta)
    w_branch_b = dense(ks[12], (L, B_OUT_W, D), B_OUT_W, beta)
    w_out = dense(ks[13], (L, D, D), D, beta)
    ln1_g = 1.0 + 0.02 * jax.random.normal(ks[14], (L, D), jnp.float32)
    ln1_b = 0.02 * jax.random.normal(ks[15], (L, D), jnp.float32)
    w_up = dense(ks[16], (L, D, D_FF), D)
    w_down = dense(ks[17], (L, D_FF, D), D_FF, beta)
    w_ple_gate = dense(ks[18], (L, D, D), D)
    b_ple_gate = 0.1 * jax.random.normal(ks[19], (L, D), jnp.float32)
    w_ple = dense(ks[20], (L, PLE_DIM, D), PLE_DIM, beta)
    ln2_g = 1.0 + 0.02 * jax.random.normal(ks[21], (L, D), jnp.float32)
    ln2_b = 0.02 * jax.random.normal(ks[22], (L, D), jnp.float32)
    return {"x": x, "p": p, "w_in": w_in, "b_gate": b_gate, "a_sink": a_sink,
            "w_branch_a": w_branch_a, "w_branch_b": w_branch_b, "w_out": w_out,
            "ln1_g": ln1_g, "ln1_b": ln1_b, "w_up": w_up, "w_down": w_down,
            "w_ple_gate": w_ple_gate, "b_ple_gate": b_ple_gate, "w_ple": w_ple,
            "ln2_g": ln2_g, "ln2_b": ln2_b}


def reference(x, p, w_in, b_gate, a_sink, w_branch_a, w_branch_b, w_out,
              ln1_g, ln1_b, w_up, w_down, w_ple_gate, b_ple_gate, w_ple,
              ln2_g, ln2_b):
    cos, sin = _rope_tables(x.shape[1])
    for i in range(DEPTH):
        h = _mixer_sublayer(x, cos, sin, w_in[i], b_gate[i], a_sink[i],
                            w_branch_a[i], w_branch_b[i], w_out[i])
        x = _layer_norm(DEEPNORM_ALPHA * x + h, ln1_g[i], ln1_b[i])
        mlp = jnp.einsum('bsf,fd->bsd', jnp.square(jax.nn.relu(jnp.einsum('bsd,df->bsf', x, w_up[i]))), w_down[i])
        gate = jax.nn.sigmoid((jnp.einsum('bsd,de->bse', x, w_ple_gate[i]) + b_ple_gate[i]).astype(jnp.float32)).astype(x.dtype)
        ple = gate * jnp.einsum('bsr,rd->bsd', p[i], w_ple[i])
        x = _layer_norm(DEEPNORM_ALPHA * x + mlp + ple, ln2_g[i], ln2_b[i])
    return x
```

```python
import functools
import math

import jax
import jax.numpy as jnp
import numpy as np
from jax import lax
from jax.experimental import pallas as pl
from jax.experimental.pallas import tpu as pltpu

D_MODEL = 1024
HEAD_DIM = 64
ROT_DIM = HEAD_DIM // 4
ROT_HALF = ROT_DIM // 2
ROPE_THETA = 500000.0

A_Q_HEADS = 8
A_KV_HEADS = 2
A_WINDOW = 128
B_GROUPS = ((128, 1), (512, 4), (2048, 16))
B_HEADS_PER_GROUP = 4

A_Q_W = A_Q_HEADS * HEAD_DIM
A_KV_W = A_KV_HEADS * HEAD_DIM
B_GROUP_W = B_HEADS_PER_GROUP * HEAD_DIM
B_W = len(B_GROUPS) * B_GROUP_W
QKV_W = A_Q_W + 2 * A_KV_W + 3 * B_W
GATE_W = 2 * D_MODEL
D_FF = 4 * D_MODEL
PLE_DIM = 256
LN_EPS = 1e-5

LANES = 128
Q_BLOCK = 128
VMEM_LIMIT_BYTES = 56 * 1024 * 1024
NEG_BIG = -1e30

_QKV_SEGMENTS = (
    (0, A_Q_W, "q"),
    (A_Q_W, 2 * A_KV_W, "kv"),
    (A_Q_W + 2 * A_KV_W, B_W, "q"),
    (A_Q_W + 2 * A_KV_W + B_W, B_W, "k"),
    (A_Q_W + 2 * A_KV_W + 2 * B_W, B_W, "v"),
)


def _rope_tables(seq):
    pos = jnp.arange(seq, dtype=jnp.float32)
    inv_freq = ROPE_THETA ** (-jnp.arange(0, ROT_DIM, 2, dtype=jnp.float32) / ROT_DIM)
    ang = pos[:, None] * inv_freq[None, :]
    cos, sin = jnp.cos(ang), jnp.sin(ang)
    zeros = jnp.zeros((seq, HEAD_DIM - ROT_DIM), jnp.float32)
    zeros_h = jnp.zeros((seq, ROT_HALF), jnp.float32)
    c = jnp.concatenate([cos, cos, jnp.ones_like(zeros)], axis=-1)
    sa = jnp.concatenate([-sin, zeros_h, zeros], axis=-1)
    sb = jnp.concatenate([zeros_h, sin, zeros], axis=-1)
    reps = LANES // HEAD_DIM
    return tuple(jnp.tile(t, (1, reps)) for t in (c, sa, sb))


def _rope_tile(t, c, sa, sb):
    up = pltpu.roll(t, LANES - ROT_HALF, axis=1)
    down = pltpu.roll(t, ROT_HALF, axis=1)
    return t * c + up * sa + down * sb


def _in_proj_kernel(x_ref, w_ref, bg_ref, c_ref, sa_ref, sb_ref, qkv_ref, gate_ref):
    xb = x_ref[...].astype(jnp.bfloat16)
    c, sa, sb = c_ref[...], sa_ref[...], sb_ref[...]
    scale = 1.0 / math.sqrt(HEAD_DIM)
    for start, width, kind in _QKV_SEGMENTS:
        acc = jnp.dot(xb, w_ref[:, start:start + width], preferred_element_type=jnp.float32)
        for t in range(width // LANES):
            tile = acc[:, t * LANES:(t + 1) * LANES]
            rotate = kind in ("q", "k") or (kind == "kv" and t == 0)
            if rotate:
                tile = _rope_tile(tile, c, sa, sb)
            if kind == "q":
                tile = tile * scale
            col = start + t * LANES
            qkv_ref[:, col:col + LANES] = tile.astype(qkv_ref.dtype)
    for j in range(GATE_W // 512):
        z = jnp.dot(xb, w_ref[:, QKV_W + j * 512:QKV_W + (j + 1) * 512],
                    preferred_element_type=jnp.float32)
        z = z + bg_ref[:, j * 512:(j + 1) * 512]
        gate_ref[:, j * 512:(j + 1) * 512] = (1.0 / (1.0 + jnp.exp(-z))).astype(gate_ref.dtype)


def _in_proj(x2d, w_in, b_gate, tables, seq, tm):
    n_tok = x2d.shape[0]
    pos_blocks = seq // tm
    const = lambda i: (0, 0)
    tab_spec = pl.BlockSpec((tm, LANES), lambda i: (i % pos_blocks, 0))
    return pl.pallas_call(
        _in_proj_kernel,
        grid=(n_tok // tm,),
        in_specs=[
            pl.BlockSpec((tm, D_MODEL), lambda i: (i, 0)),
            pl.BlockSpec((D_MODEL, QKV_W + GATE_W), const, pipeline_mode=pl.Buffered(1)),
            pl.BlockSpec((1, GATE_W), const),
            tab_spec, tab_spec, tab_spec,
        ],
        out_specs=[
            pl.BlockSpec((tm, QKV_W), lambda i: (i, 0)),
            pl.BlockSpec((tm, GATE_W), lambda i: (i, 0)),
        ],
        out_shape=[
            jax.ShapeDtypeStruct((n_tok, QKV_W), jnp.bfloat16),
            jax.ShapeDtypeStruct((n_tok, GATE_W), jnp.bfloat16),
        ],
        compiler_params=pltpu.CompilerParams(
            dimension_semantics=("arbitrary",), vmem_limit_bytes=VMEM_LIMIT_BYTES),
        name="in_proj",
    )(x2d, w_in, b_gate, *tables)


def _attn_kernel(*refs, length, n_q_heads, n_kv_heads, win, has_sink, emit_lse):
    refs = list(refs)
    q_ref, k_ref, v_ref = refs[:3]
    pos = 3
    sink_ref = None
    if has_sink:
        sink_ref = refs[pos]
        pos += 1
    o_ref = refs[pos]
    lse_ref = refs[pos + 1] if emit_lse else None

    nk = min(length, Q_BLOCK + 2 * win)
    grp = n_q_heads // n_kv_heads
    n_blocks = length // Q_BLOCK
    rel = (lax.broadcasted_iota(jnp.int32, (Q_BLOCK, nk), 0)
           - lax.broadcasted_iota(jnp.int32, (Q_BLOCK, nk), 1))

    def block(i, carry):
        q0 = pl.multiple_of(i * Q_BLOCK, Q_BLOCK)
        k0 = pl.multiple_of(jnp.clip(q0 - win, 0, length - nk), 64)
        valid = jnp.abs(rel + (q0 - k0)) <= win
        outs, lses = [], []
        for h in range(n_q_heads):
            kvh = h // grp
            q = q_ref[pl.ds(q0, Q_BLOCK), h * HEAD_DIM:(h + 1) * HEAD_DIM]
            k = k_ref[pl.ds(k0, nk), kvh * HEAD_DIM:(kvh + 1) * HEAD_DIM]
            v = v_ref[pl.ds(k0, nk), kvh * HEAD_DIM:(kvh + 1) * HEAD_DIM]
            s = lax.dot_general(q, k, (((1,), (1,)), ((), ())),
                                preferred_element_type=jnp.float32)
            s = jnp.where(valid, s, NEG_BIG)
            m = jnp.max(s, axis=1, keepdims=True)
            if has_sink:
                m = jnp.maximum(m, sink_ref[h])
            e = jnp.exp(s - m)
            denom = jnp.sum(e, axis=1, keepdims=True)
            if has_sink:
                denom = denom + jnp.exp(sink_ref[h] - m)
            o = jnp.dot(e.astype(jnp.bfloat16), v, preferred_element_type=jnp.float32) / denom
            outs.append(o)
            if emit_lse:
                lses.append(jnp.broadcast_to(m + jnp.log(denom), (Q_BLOCK, HEAD_DIM)))
        o_ref[pl.ds(q0, Q_BLOCK), :] = jnp.concatenate(outs, axis=1).astype(o_ref.dtype)
        if emit_lse:
            lse_ref[pl.ds(q0, Q_BLOCK), :] = jnp.concatenate(lses, axis=1)
        return carry

    lax.fori_loop(0, n_blocks, block, 0)


def _attn_a(qkv3, sink):
    bsz, seq, _ = qkv3.shape
    kern = functools.partial(_attn_kernel, length=seq, n_q_heads=A_Q_HEADS,
                             n_kv_heads=A_KV_HEADS, win=A_WINDOW, has_sink=True, emit_lse=False)
    k_blk = A_Q_W // A_KV_W
    return pl.pallas_call(
        kern,
        grid=(bsz,),
        in_specs=[
            pl.BlockSpec((None, seq, A_Q_W), lambda b: (b, 0, 0)),
            pl.BlockSpec((None, seq, A_KV_W), lambda b: (b, 0, k_blk)),
            pl.BlockSpec((None, seq, A_KV_W), lambda b: (b, 0, k_blk + 1)),
            pl.BlockSpec(memory_space=pltpu.SMEM),
        ],
        out_specs=pl.BlockSpec((None, seq, A_Q_W), lambda b: (b, 0, 0)),
        out_shape=jax.ShapeDtypeStruct((bsz, seq, A_Q_W), jnp.bfloat16),
        compiler_params=pltpu.CompilerParams(
            dimension_semantics=("arbitrary",), vmem_limit_bytes=VMEM_LIMIT_BYTES),
        name="attn_a",
    )(qkv3, qkv3, qkv3, sink)


def _attn_b(qkv3, group):
    bsz, seq, _ = qkv3.shape
    window, r = B_GROUPS[group]
    win = (window // 2) // r
    length = seq // r
    view = qkv3.reshape(bsz, length, r * QKV_W)
    per_class = QKV_W // B_GROUP_W
    q_off = (A_Q_W + 2 * A_KV_W) // B_GROUP_W + group
    k_off = q_off + B_W // B_GROUP_W
    v_off = k_off + B_W // B_GROUP_W
    kern = functools.partial(_attn_kernel, length=length, n_q_heads=B_HEADS_PER_GROUP,
                             n_kv_heads=B_HEADS_PER_GROUP, win=win, has_sink=False, emit_lse=True)

    def in_spec(off):
        return pl.BlockSpec((None, length, B_GROUP_W), lambda b, c: (b, 0, c * per_class + off))

    out_spec = pl.BlockSpec((None, length, B_GROUP_W), lambda b, c: (b, 0, c))
    o, lse = pl.pallas_call(
        kern,
        grid=(bsz, r),
        in_specs=[in_spec(q_off), in_spec(k_off), in_spec(v_off)],
        out_specs=[out_spec, out_spec],
        out_shape=[
            jax.ShapeDtypeStruct((bsz, length, r * B_GROUP_W), jnp.bfloat16),
            jax.ShapeDtypeStruct((bsz, length, r * B_GROUP_W), jnp.float32),
        ],
        compiler_params=pltpu.CompilerParams(
            dimension_semantics=("arbitrary", "arbitrary"), vmem_limit_bytes=VMEM_LIMIT_BYTES),
        name=f"attn_b{group}",
    )(view, view, view)
    return o.reshape(bsz * seq, B_GROUP_W), lse.reshape(bsz * seq, B_GROUP_W)


def _layer_norm(y, g, b):
    mu = jnp.mean(y, axis=-1, keepdims=True)
    d = y - mu
    var = jnp.mean(d * d, axis=-1, keepdims=True)
    return d * lax.rsqrt(var + LN_EPS) * g + b


def _post_kernel(x_ref, oa_ref, o0_ref, o1_ref, o2_ref, l0_ref, l1_ref, l2_ref, gate_ref, p_ref,
                 wa_ref, wb_ref, wo_ref, g1_ref, b1_ref, wup_ref, wdn_ref, wpg_ref, bpg_ref,
                 wple_ref, g2_ref, b2_ref, out_ref, *, alpha):
    f32, bf16 = jnp.float32, jnp.bfloat16
    l0, l1, l2 = l0_ref[...], l1_ref[...], l2_ref[...]
    mx = jnp.maximum(jnp.maximum(l0, l1), l2)
    e0, e1, e2 = jnp.exp(l0 - mx), jnp.exp(l1 - mx), jnp.exp(l2 - mx)
    ob = (e0 * o0_ref[...].astype(f32) + e1 * o1_ref[...].astype(f32)
          + e2 * o2_ref[...].astype(f32)) / (e0 + e1 + e2)
    ya = jnp.dot(oa_ref[...], wa_ref[...], preferred_element_type=f32)
    yb = jnp.dot(ob.astype(bf16), wb_ref[...], preferred_element_type=f32)
    ga = gate_ref[:, :D_MODEL].astype(f32)
    gb = gate_ref[:, D_MODEL:].astype(f32)
    merged = (ga * ya + gb * yb).astype(bf16)
    h = jnp.dot(merged, wo_ref[...], preferred_element_type=f32)
    x1 = _layer_norm(alpha * x_ref[...] + h, g1_ref[...], b1_ref[...])
    x1b = x1.astype(bf16)
    up = jnp.dot(x1b, wup_ref[...], preferred_element_type=f32)
    act = jnp.square(jnp.maximum(up, 0.0)).astype(bf16)
    mlp = jnp.dot(act, wdn_ref[...], preferred_element_type=f32)
    zg = jnp.dot(x1b, wpg_ref[...], preferred_element_type=f32) + bpg_ref[...]
    gate = 1.0 / (1.0 + jnp.exp(-zg))
    ple = gate * jnp.dot(p_ref[...].astype(bf16), wple_ref[...], preferred_element_type=f32)
    out_ref[...] = _layer_norm(alpha * x1 + mlp + ple, g2_ref[...], b2_ref[...])


def _post(x2d, oa, ob_parts, lse_parts, gates, p_all, layer, weights, alpha, tm):
    n_tok = x2d.shape[0]
    const = lambda i: (0, 0)

    def rows(width):
        return pl.BlockSpec((tm, width), lambda i: (i, 0))

    def resident(shape):
        return pl.BlockSpec(shape, const, pipeline_mode=pl.Buffered(1))

    vec = pl.BlockSpec((1, D_MODEL), const)
    (wa, wb, wo, g1, b1, wup, wdn, wpg, bpg, wple, g2, b2) = weights
    return pl.pallas_call(
        functools.partial(_post_kernel, alpha=alpha),
        grid=(n_tok // tm,),
        in_specs=[
            rows(D_MODEL), rows(A_Q_W),
            rows(B_GROUP_W), rows(B_GROUP_W), rows(B_GROUP_W),
            rows(B_GROUP_W), rows(B_GROUP_W), rows(B_GROUP_W),
            rows(GATE_W),
            pl.BlockSpec((None, tm, PLE_DIM), lambda i: (layer, i, 0)),
            resident((A_Q_W, D_MODEL)), resident((B_GROUP_W, D_MODEL)),
            resident((D_MODEL, D_MODEL)), vec, vec,
            resident((D_MODEL, D_FF)), resident((D_FF, D_MODEL)),
            resident((D_MODEL, D_MODEL)), vec,
            resident((PLE_DIM, D_MODEL)), vec, vec,
        ],
        out_specs=rows(D_MODEL),
        out_shape=jax.ShapeDtypeStruct((n_tok, D_MODEL), jnp.float32),
        compiler_params=pltpu.CompilerParams(
            dimension_semantics=("arbitrary",), vmem_limit_bytes=VMEM_LIMIT_BYTES),
        name="post",
    )(x2d, oa, *ob_parts, *lse_parts, gates, p_all, wa, wb, wo, g1, b1, wup, wdn, wpg, bpg,
      wple, g2, b2)


def kernel(x, p, w_in, b_gate, a_sink, w_branch_a, w_branch_b, w_out, ln1_g, ln1_b, w_up, w_down,
           w_ple_gate, b_ple_gate, w_ple, ln2_g, ln2_b):
    bsz, seq, d_model = x.shape
    depth = w_in.shape[0]
    assert d_model == D_MODEL and w_in.shape[2] == QKV_W + GATE_W
    assert seq % (Q_BLOCK * max(r for _, r in B_GROUPS)) == 0
    n_tok = bsz * seq
    alpha = (2 * depth) ** 0.25
    bf16 = jnp.bfloat16
    tables = _rope_tables(seq)
    p_all = p.reshape(depth, n_tok, PLE_DIM)
    x2d = x.reshape(n_tok, d_model)
    for i in range(depth):
        qkv, gates = _in_proj(x2d, w_in[i].astype(bf16), b_gate[i][None, :], tables, seq, tm=512)
        qkv3 = qkv.reshape(bsz, seq, QKV_W)
        oa = _attn_a(qkv3, a_sink[i]).reshape(n_tok, A_Q_W)
        parts = [_attn_b(qkv3, g) for g in range(len(B_GROUPS))]
        weights = (w_branch_a[i].astype(bf16), w_branch_b[i].astype(bf16), w_out[i].astype(bf16),
                   ln1_g[i][None, :], ln1_b[i][None, :], w_up[i].astype(bf16),
                   w_down[i].astype(bf16), w_ple_gate[i].astype(bf16), b_ple_gate[i][None, :],
                   w_ple[i].astype(bf16), ln2_g[i][None, :], ln2_b[i][None, :])
        x2d = _post(x2d, oa, [o for o, _ in parts], [l for _, l in parts], gates, p_all, i,
                    weights, alpha, tm=256)
    return x2d.reshape(bsz, seq, d_model)
```

```python
import functools
import math

import jax
import jax.numpy as jnp
from jax import lax
from jax.experimental import pallas as pl
from jax.experimental.pallas import tpu as pltpu

D_MODEL = 1024
HEAD_DIM = 64
ROT_DIM = HEAD_DIM // 4
ROT_HALF = ROT_DIM // 2
ROPE_THETA = 500000.0

A_Q_HEADS = 8
A_KV_HEADS = 2
A_WINDOW = 128
B_GROUPS = ((128, 1), (512, 4), (2048, 16))
N_GROUPS = len(B_GROUPS)
B_HEADS_PER_GROUP = 4
DILATIONS = tuple(r for _, r in B_GROUPS)

A_Q_W = A_Q_HEADS * HEAD_DIM
A_KV_W = A_KV_HEADS * HEAD_DIM
B_GROUP_W = B_HEADS_PER_GROUP * HEAD_DIM
B_W = N_GROUPS * B_GROUP_W
GATE_W = 2 * D_MODEL
D_FF = 4 * D_MODEL
PLE_DIM = 256
LN_EPS = 1e-5

LANES = 128
SUBLANES = 8
TOKEN_BLOCK = 128
VMEM_LIMIT_BYTES = 56 * 1024 * 1024
NEG_BIG = -1e30

IN_PROJ_ROWS = 512
POST_ROWS = 256

_NAT_KA = 0
_NAT_KB = A_KV_W
_NAT_GATE = A_KV_W + B_W
NAT_W = _NAT_GATE + GATE_W
_T_QA = 0
_T_VA = A_Q_W
_T_B = A_Q_W + A_KV_W
T_W = _T_B + 2 * B_W


def _rope_tables(seq):
    pos = jnp.arange(seq, dtype=jnp.float32)
    inv_freq = ROPE_THETA ** (-jnp.arange(0, ROT_DIM, 2, dtype=jnp.float32) / ROT_DIM)
    ang = pos[:, None] * inv_freq[None, :]
    cos, sin = jnp.cos(ang), jnp.sin(ang)
    zeros = jnp.zeros((seq, HEAD_DIM - ROT_DIM), jnp.float32)
    zeros_h = jnp.zeros((seq, ROT_HALF), jnp.float32)
    reps = LANES // HEAD_DIM
    c = jnp.tile(jnp.concatenate([cos, cos, jnp.ones_like(zeros)], axis=-1), (1, reps))
    sa = jnp.tile(jnp.concatenate([-sin, zeros_h, zeros], axis=-1), (1, reps))
    sb = jnp.tile(jnp.concatenate([zeros_h, sin, zeros], axis=-1), (1, reps))

    def orders(t):
        return jnp.stack([t.reshape(seq // r, r, -1).transpose(1, 0, 2).reshape(seq, -1)
                          for r in DILATIONS])

    c, sa, sb, cos_o, sin_o = (orders(t) for t in (c, sa, sb, cos, sin))
    return c, sa, sb, cos_o.transpose(0, 2, 1), sin_o.transpose(0, 2, 1)


def _in_proj_weights(w):
    qa, ka, va, qb, kb, vb, gl = jnp.split(
        w, (A_Q_W, A_Q_W + A_KV_W, A_Q_W + 2 * A_KV_W, A_Q_W + 2 * A_KV_W + B_W,
            A_Q_W + 2 * A_KV_W + 2 * B_W, A_Q_W + 2 * A_KV_W + 3 * B_W), axis=1)
    scale = 1.0 / math.sqrt(HEAD_DIM)
    w_nat = jnp.concatenate([ka, kb, gl], axis=1).astype(jnp.bfloat16)
    cols = [qa * scale, va]
    for g in range(N_GROUPS):
        cols += [qb[:, g * B_GROUP_W:(g + 1) * B_GROUP_W] * scale,
                 vb[:, g * B_GROUP_W:(g + 1) * B_GROUP_W]]
    w_t = jnp.concatenate(cols, axis=1).T.astype(jnp.bfloat16)
    return w_nat, w_t


def _store_class_major(dst_refs, slab_ref, val):
    rows, width = val.shape
    n_slabs = width // LANES
    for s in range(n_slabs):
        slab_ref[s] = val[:, s * LANES:(s + 1) * LANES]
    for r, dst_ref in zip(DILATIONS[1:], dst_refs):
        n = rows // r
        for c in range(r):
            dst_ref[c] = jnp.concatenate(
                [slab_ref[s, pl.ds(c, n, stride=r), :] for s in range(n_slabs)],
                axis=1).astype(dst_ref.dtype)


def _load_token_order(src_ref, slab_ref, r):
    n, width = src_ref.shape[1:]
    n_slabs = width // LANES
    for c in range(r):
        blk = src_ref[c].astype(jnp.float32)
        for s in range(n_slabs):
            slab_ref[s, pl.ds(c, n, stride=r), :] = blk[:, s * LANES:(s + 1) * LANES]
    return jnp.concatenate([slab_ref[s] for s in range(n_slabs)], axis=1)


def _reorder_kernel(x_ref, xb_ref, *rest):
    xp_refs, slab_ref = rest[:-1], rest[-1]
    x = x_ref[...]
    xb_ref[...] = x.astype(xb_ref.dtype)
    _store_class_major(xp_refs, slab_ref, x)


def _x_order_specs(bsz, seq, tm):
    per_seq = seq // tm
    specs = [pl.BlockSpec((tm, D_MODEL), lambda i: (i, 0))]
    shapes = [jax.ShapeDtypeStruct((bsz * seq, D_MODEL), jnp.bfloat16)]
    for r in DILATIONS[1:]:
        specs.append(pl.BlockSpec((None, r, tm // r, D_MODEL),
                                  lambda i: (i // per_seq, 0, i % per_seq, 0)))
        shapes.append(jax.ShapeDtypeStruct((bsz, r, seq // r, D_MODEL), jnp.bfloat16))
    return specs, shapes


def _slab_scratch(rows, width):
    return pltpu.VMEM((width // LANES, rows, LANES), jnp.float32)


def _reorder(x2d, bsz, seq):
    tm = POST_ROWS
    specs, shapes = _x_order_specs(bsz, seq, tm)
    return pl.pallas_call(
        _reorder_kernel,
        grid=(bsz * seq // tm,),
        in_specs=[pl.BlockSpec((tm, D_MODEL), lambda i: (i, 0))],
        out_specs=specs,
        out_shape=shapes,
        scratch_shapes=[_slab_scratch(tm, D_MODEL)],
        compiler_params=pltpu.CompilerParams(
            dimension_semantics=("arbitrary",), vmem_limit_bytes=VMEM_LIMIT_BYTES),
        name="reorder",
    )(x2d)


def _rope_lanes(t, c, sa, sb):
    up = pltpu.roll(t, LANES - ROT_HALF, axis=1)
    down = pltpu.roll(t, ROT_HALF, axis=1)
    return t * c + up * sa + down * sb


def _rope_sublanes(t, cos, sin, n_heads):
    parts = []
    for h in range(n_heads):
        a = t[h * HEAD_DIM:h * HEAD_DIM + ROT_HALF]
        b = t[h * HEAD_DIM + ROT_HALF:h * HEAD_DIM + ROT_DIM]
        parts += [a * cos - b * sin, b * cos + a * sin, t[h * HEAD_DIM + ROT_DIM:(h + 1) * HEAD_DIM]]
    return jnp.concatenate(parts, axis=0)


def _store_token_blocks(dst_ref, val):
    for j in range(val.shape[1] // LANES):
        dst_ref[j] = val[:, j * LANES:(j + 1) * LANES].astype(dst_ref.dtype)


def _in_proj_kernel(*refs):
    x_refs = refs[:N_GROUPS]
    wn_ref, wt_ref, bg_ref, cl_ref, sal_ref, sbl_ref, ct_ref, st_ref = refs[N_GROUPS:N_GROUPS + 8]
    outs = refs[N_GROUPS + 8:]
    ka_ref, qat_ref, vat_ref = outs[:3]
    b_refs = [outs[3 + 3 * g:6 + 3 * g] for g in range(N_GROUPS)]
    gate_ref = outs[3 + 3 * N_GROUPS]
    f32 = jnp.float32
    nt = (((1,), (1,)), ((), ()))

    def k_rope(acc, order):
        c, sa, sb = cl_ref[order], sal_ref[order], sbl_ref[order]
        return jnp.concatenate(
            [_rope_lanes(acc[:, t * LANES:(t + 1) * LANES], c, sa, sb)
             for t in range(acc.shape[1] // LANES)], axis=1)

    x0 = x_refs[0][...]
    acc = jnp.dot(x0, wn_ref[:, _NAT_KA:_NAT_KA + A_KV_W], preferred_element_type=f32)
    ka_ref[...] = k_rope(acc, 0).astype(ka_ref.dtype)
    acc = lax.dot_general(wt_ref[_T_QA:_T_QA + A_Q_W, :], x0, nt, preferred_element_type=f32)
    _store_token_blocks(qat_ref, _rope_sublanes(acc, ct_ref[0], st_ref[0], A_Q_HEADS))
    acc = lax.dot_general(wt_ref[_T_VA:_T_VA + A_KV_W, :], x0, nt, preferred_element_type=f32)
    _store_token_blocks(vat_ref, acc)
    for g in range(N_GROUPS):
        xg = x_refs[g][...]
        k_ref, qt_ref, vt_ref = b_refs[g]
        col = _NAT_KB + g * B_GROUP_W
        acc = jnp.dot(xg, wn_ref[:, col:col + B_GROUP_W], preferred_element_type=f32)
        k_ref[...] = k_rope(acc, g).astype(k_ref.dtype)
        row = _T_B + 2 * g * B_GROUP_W
        acc = lax.dot_general(wt_ref[row:row + B_GROUP_W, :], xg, nt, preferred_element_type=f32)
        _store_token_blocks(qt_ref, _rope_sublanes(acc, ct_ref[g], st_ref[g], B_HEADS_PER_GROUP))
        acc = lax.dot_general(wt_ref[row + B_GROUP_W:row + 2 * B_GROUP_W, :], xg, nt,
                              preferred_element_type=f32)
        _store_token_blocks(vt_ref, acc)
    chunk = 512
    for j in range(GATE_W // chunk):
        z = jnp.dot(x0, wn_ref[:, _NAT_GATE + j * chunk:_NAT_GATE + (j + 1) * chunk],
                    preferred_element_type=f32)
        z = z + bg_ref[:, j * chunk:(j + 1) * chunk]
        gate_ref[:, j * chunk:(j + 1) * chunk] = (1.0 / (1.0 + jnp.exp(-z))).astype(gate_ref.dtype)


def _in_proj(x_orders, w_nat, w_t, b_gate, tables, seq):
    tm = IN_PROJ_ROWS
    n_tok = x_orders[0].shape[0]
    per_seq = seq // tm
    blocks = tm // TOKEN_BLOCK
    const = lambda i: (0, 0)
    rows = lambda w: pl.BlockSpec((tm, w), lambda i: (i, 0))
    tblocks = lambda r: pl.BlockSpec((blocks, r, TOKEN_BLOCK), lambda i: (i, 0, 0))
    lane_tab = pl.BlockSpec((N_GROUPS, tm, LANES), lambda i: (0, i % per_seq, 0))
    sub_tab = pl.BlockSpec((N_GROUPS, ROT_HALF, tm), lambda i: (0, 0, i % per_seq))
    bf16 = jnp.bfloat16
    nat = lambda w: jax.ShapeDtypeStruct((n_tok, w), bf16)
    tsp = lambda r: jax.ShapeDtypeStruct((n_tok // TOKEN_BLOCK, r, TOKEN_BLOCK), bf16)
    out_specs = [rows(A_KV_W), tblocks(A_Q_W), tblocks(A_KV_W)]
    out_shape = [nat(A_KV_W), tsp(A_Q_W), tsp(A_KV_W)]
    for _ in range(N_GROUPS):
        out_specs += [rows(B_GROUP_W), tblocks(B_GROUP_W), tblocks(B_GROUP_W)]
        out_shape += [nat(B_GROUP_W), tsp(B_GROUP_W), tsp(B_GROUP_W)]
    out_specs.append(rows(GATE_W))
    out_shape.append(nat(GATE_W))
    return pl.pallas_call(
        _in_proj_kernel,
        grid=(n_tok // tm,),
        in_specs=[rows(D_MODEL)] * N_GROUPS + [
            pl.BlockSpec((D_MODEL, NAT_W), const, pipeline_mode=pl.Buffered(1)),
            pl.BlockSpec((T_W, D_MODEL), const, pipeline_mode=pl.Buffered(1)),
            pl.BlockSpec((1, GATE_W), const),
            lane_tab, lane_tab, lane_tab, sub_tab, sub_tab,
        ],
        out_specs=out_specs,
        out_shape=out_shape,
        compiler_params=pltpu.CompilerParams(
            dimension_semantics=("arbitrary",), vmem_limit_bytes=VMEM_LIMIT_BYTES),
        name="in_proj",
    )(*x_orders, w_nat, w_t, b_gate, *tables)


def _attn_kernel(*refs, blocks_per_seq, n_q_heads, n_kv_heads, win, has_sink, emit_lse):
    refs = list(refs)
    qt_ref, k_ref, vt_ref = refs[:3]
    pos = 3
    sink_ref = None
    if has_sink:
        sink_ref = refs[pos]
        pos += 1
    o_ref = refs[pos]
    lse_ref = refs[pos + 1] if emit_lse else None

    n_blocks = qt_ref.shape[0]
    grp = n_q_heads // n_kv_heads
    tb = TOKEN_BLOCK
    offsets = (0,) if blocks_per_seq == 1 else (-1, 0, 1)
    f32, bf16 = jnp.float32, jnp.bfloat16
    rel = (lax.broadcasted_iota(jnp.int32, (tb, 2 * tb), 1) % tb
           - lax.broadcasted_iota(jnp.int32, (tb, 2 * tb), 0))
    dist = {t: jnp.abs(rel - t * tb) for t in offsets}
    lane = lax.broadcasted_iota(jnp.int32, (1, 2 * tb), 1)
    zeros = jnp.zeros((HEAD_DIM, tb), bf16)

    def block(i, carry):
        local = i % blocks_per_seq
        q0 = pl.multiple_of(i * tb, tb)
        tiles = {}
        for t in offsets:
            ok = jnp.logical_and(local + t >= 0, local + t < blocks_per_seq)
            tiles[t] = (pl.multiple_of(jnp.where(ok, i + t, i) * tb, tb),
                        jnp.where(ok, i + t, i), jnp.where(ok, win, -1))
        for j in range(n_q_heads // 2):
            ha, hb = 2 * j, 2 * j + 1
            ka, kb = ha // grp, hb // grp
            assert ka // 2 == kb // 2
            qa = qt_ref[i, ha * HEAD_DIM:(ha + 1) * HEAD_DIM, :]
            qb = qt_ref[i, hb * HEAD_DIM:(hb + 1) * HEAD_DIM, :]
            halves = [jnp.concatenate([qa if ka % 2 == u else zeros,
                                       qb if kb % 2 == u else zeros], axis=1) for u in (0, 1)]
            rhs = jnp.concatenate(halves, axis=0)
            s = {}
            for t in offsets:
                k0, _, reach = tiles[t]
                kt = k_ref[pl.ds(k0, tb), (ka // 2) * LANES:(ka // 2 + 1) * LANES]
                st = jnp.dot(kt, rhs, preferred_element_type=f32)
                if not (t == 0 and win >= tb - 1):
                    st = jnp.where(dist[t] <= reach, st, NEG_BIG)
                s[t] = st
            m = functools.reduce(jnp.maximum, [jnp.max(s[t], axis=0, keepdims=True)
                                               for t in offsets])
            if has_sink:
                sink = jnp.where(lane < tb, sink_ref[ha], sink_ref[hb])
                m = jnp.maximum(m, sink)
            e = {t: jnp.exp(s[t] - m) for t in offsets}
            denom = functools.reduce(jnp.add, [jnp.sum(e[t], axis=0, keepdims=True)
                                               for t in offsets])
            if has_sink:
                denom = denom + jnp.exp(sink - m)
            acc = None
            for t in offsets:
                _, blk, _ = tiles[t]
                p = e[t].astype(bf16)
                if ka == kb:
                    part = jnp.dot(vt_ref[blk, ka * HEAD_DIM:(ka + 1) * HEAD_DIM, :], p,
                                   preferred_element_type=f32)
                else:
                    part = jnp.concatenate(
                        [jnp.dot(vt_ref[blk, ka * HEAD_DIM:(ka + 1) * HEAD_DIM, :], p[:, :tb],
                                 preferred_element_type=f32),
                         jnp.dot(vt_ref[blk, kb * HEAD_DIM:(kb + 1) * HEAD_DIM, :], p[:, tb:],
                                 preferred_element_type=f32)], axis=1)
                acc = part if acc is None else acc + part
            ot = acc / denom
            pair = jnp.concatenate([ot[:, :tb], ot[:, tb:]], axis=0)
            o_ref[pl.ds(q0, tb), j * LANES:(j + 1) * LANES] = pair.T.astype(o_ref.dtype)
            if emit_lse:
                lse = m + jnp.log(denom)
                both = jnp.concatenate([jnp.broadcast_to(lse[:, :tb], (HEAD_DIM, tb)),
                                        jnp.broadcast_to(lse[:, tb:], (HEAD_DIM, tb))], axis=0)
                lse_ref[pl.ds(q0, tb), j * LANES:(j + 1) * LANES] = both.T
        return carry

    lax.fori_loop(0, n_blocks, block, 0)


def _attn(qt, k, vt, sink, *, seq, blocks_per_seq, n_q_heads, n_kv_heads, win, emit_lse, name):
    n_tok = k.shape[0]
    blocks = seq // TOKEN_BLOCK
    qw, kw = n_q_heads * HEAD_DIM, n_kv_heads * HEAD_DIM
    kern = functools.partial(_attn_kernel, blocks_per_seq=blocks_per_seq, n_q_heads=n_q_heads,
                             n_kv_heads=n_kv_heads, win=win, has_sink=sink is not None,
                             emit_lse=emit_lse)
    in_specs = [pl.BlockSpec((blocks, qw, TOKEN_BLOCK), lambda b: (b, 0, 0)),
                pl.BlockSpec((seq, kw), lambda b: (b, 0)),
                pl.BlockSpec((blocks, kw, TOKEN_BLOCK), lambda b: (b, 0, 0))]
    args = [qt, k, vt]
    if sink is not None:
        in_specs.append(pl.BlockSpec(memory_space=pltpu.SMEM))
        args.append(sink)
    out_specs = [pl.BlockSpec((seq, qw), lambda b: (b, 0))]
    out_shape = [jax.ShapeDtypeStruct((n_tok, qw), jnp.bfloat16)]
    if emit_lse:
        out_specs.append(pl.BlockSpec((seq, qw), lambda b: (b, 0)))
        out_shape.append(jax.ShapeDtypeStruct((n_tok, qw), jnp.float32))
    return pl.pallas_call(
        kern,
        grid=(n_tok // seq,),
        in_specs=in_specs,
        out_specs=out_specs,
        out_shape=out_shape,
        compiler_params=pltpu.CompilerParams(
            dimension_semantics=("arbitrary",), vmem_limit_bytes=VMEM_LIMIT_BYTES),
        name=name,
    )(*args)


def _layer_norm(y, g, b):
    mu = jnp.mean(y, axis=-1, keepdims=True)
    d = y - mu
    var = jnp.mean(d * d, axis=-1, keepdims=True)
    return d * lax.rsqrt(var + LN_EPS) * g + b


def _post_kernel(*refs, alpha, emit_orders):
    n_in = 2 + 2 * N_GROUPS + 2 + 12
    (x_ref, oa_ref) = refs[:2]
    o_refs = refs[2:2 + N_GROUPS]
    l_refs = refs[2 + N_GROUPS:2 + 2 * N_GROUPS]
    gate_ref, p_ref = refs[2 + 2 * N_GROUPS:4 + 2 * N_GROUPS]
    (wa_ref, wb_ref, wo_ref, g1_ref, b1_ref, wup_ref, wdn_ref, wpg_ref, bpg_ref, wple_ref,
     g2_ref, b2_ref) = refs[4 + 2 * N_GROUPS:n_in]
    out_ref = refs[n_in]
    n_order_outs = N_GROUPS if emit_orders else 0
    order_refs = refs[n_in + 1:n_in + 1 + n_order_outs]
    scratch = refs[n_in + 1 + n_order_outs:]
    f32, bf16 = jnp.float32, jnp.bfloat16

    os_, ls_ = [o_refs[0][...].astype(f32)], [l_refs[0][...]]
    for g in range(1, N_GROUPS):
        os_.append(_load_token_order(o_refs[g], scratch[2 * g - 1], DILATIONS[g]))
        ls_.append(_load_token_order(l_refs[g], scratch[2 * g], DILATIONS[g]))
    mx = functools.reduce(jnp.maximum, ls_)
    es = [jnp.exp(l - mx) for l in ls_]
    ob = functools.reduce(jnp.add, [e * o for e, o in zip(es, os_)]) / functools.reduce(jnp.add, es)
    ya = jnp.dot(oa_ref[...], wa_ref[...], preferred_element_type=f32)
    yb = jnp.dot(ob.astype(bf16), wb_ref[...], preferred_element_type=f32)
    ga = gate_ref[:, :D_MODEL].astype(f32)
    gb = gate_ref[:, D_MODEL:].astype(f32)
    merged = (ga * ya + gb * yb).astype(bf16)
    h = jnp.dot(merged, wo_ref[...], preferred_element_type=f32)
    x1 = _layer_norm(alpha * x_ref[...] + h, g1_ref[...], b1_ref[...])
    x1b = x1.astype(bf16)
    up = jnp.dot(x1b, wup_ref[...], preferred_element_type=f32)
    act = jnp.square(jnp.maximum(up, 0.0)).astype(bf16)
    mlp = jnp.dot(act, wdn_ref[...], preferred_element_type=f32)
    zg = jnp.dot(x1b, wpg_ref[...], preferred_element_type=f32) + bpg_ref[...]
    gate = 1.0 / (1.0 + jnp.exp(-zg))
    ple = gate * jnp.dot(p_ref[...].astype(bf16), wple_ref[...], preferred_element_type=f32)
    x2 = _layer_norm(alpha * x1 + mlp + ple, g2_ref[...], b2_ref[...])
    out_ref[...] = x2
    if emit_orders:
        order_refs[0][...] = x2.astype(bf16)
        _store_class_major(order_refs[1:], scratch[0], x2)


def _post(x2d, oa, o_parts, lse_parts, gates, p_all, layer, weights, alpha, bsz, seq, emit_orders):
    tm = POST_ROWS
    n_tok = x2d.shape[0]
    per_seq = seq // tm
    const = lambda i: (0, 0)
    rows = lambda w: pl.BlockSpec((tm, w), lambda i: (i, 0))
    resident = lambda shape: pl.BlockSpec(shape, const, pipeline_mode=pl.Buffered(1))
    vec = pl.BlockSpec((1, D_MODEL), const)

    def grouped(g):
        r = DILATIONS[g]
        if r == 1:
            return rows(B_GROUP_W)
        return pl.BlockSpec((None, r, tm // r, B_GROUP_W),
                            lambda i: (i // per_seq, 0, i % per_seq, 0))

    def as_grouped(a, g):
        r = DILATIONS[g]
        return a if r == 1 else a.reshape(bsz, r, seq // r, B_GROUP_W)

    out_specs = [rows(D_MODEL)]
    out_shape = [jax.ShapeDtypeStruct((n_tok, D_MODEL), jnp.float32)]
    if emit_orders:
        specs, shapes = _x_order_specs(bsz, seq, tm)
        out_specs += specs
        out_shape += shapes
    scratch_shapes = [_slab_scratch(tm, D_MODEL)]
    for g in range(1, N_GROUPS):
        scratch_shapes += [_slab_scratch(tm, B_GROUP_W)] * 2
    (wa, wb, wo, g1, b1, wup, wdn, wpg, bpg, wple, g2, b2) = weights
    return pl.pallas_call(
        functools.partial(_post_kernel, alpha=alpha, emit_orders=emit_orders),
        grid=(n_tok // tm,),
        in_specs=[rows(D_MODEL), rows(A_Q_W)]
        + [grouped(g) for g in range(N_GROUPS)] * 2
        + [rows(GATE_W), pl.BlockSpec((None, tm, PLE_DIM), lambda i: (layer, i, 0)),
           resident((A_Q_W, D_MODEL)), resident((B_GROUP_W, D_MODEL)),
           resident((D_MODEL, D_MODEL)), vec, vec,
           resident((D_MODEL, D_FF)), resident((D_FF, D_MODEL)),
           resident((D_MODEL, D_MODEL)), vec,
           resident((PLE_DIM, D_MODEL)), vec, vec],
        out_specs=out_specs,
        out_shape=out_shape,
        scratch_shapes=scratch_shapes,
        compiler_params=pltpu.CompilerParams(
            dimension_semantics=("arbitrary",), vmem_limit_bytes=VMEM_LIMIT_BYTES),
        name="post",
    )(x2d, oa, *[as_grouped(o, g) for g, o in enumerate(o_parts)],
      *[as_grouped(l, g) for g, l in enumerate(lse_parts)], gates, p_all,
      wa, wb, wo, g1, b1, wup, wdn, wpg, bpg, wple, g2, b2)


def kernel(x, p, w_in, b_gate, a_sink, w_branch_a, w_branch_b, w_out, ln1_g, ln1_b, w_up, w_down,
           w_ple_gate, b_ple_gate, w_ple, ln2_g, ln2_b):
    bsz, seq, d_model = x.shape
    depth = w_in.shape[0]
    assert d_model == D_MODEL and w_in.shape[2] == NAT_W + T_W
    assert seq % (TOKEN_BLOCK * max(DILATIONS)) == 0 and seq % IN_PROJ_ROWS == 0
    assert POST_ROWS % (2 * SUBLANES * max(DILATIONS)) == 0
    n_tok = bsz * seq
    alpha = (2 * depth) ** 0.25
    bf16 = jnp.bfloat16
    tables = _rope_tables(seq)
    p_all = p.reshape(depth, n_tok, PLE_DIM)
    x2d = x.reshape(n_tok, d_model)
    x_orders = _reorder(x2d, bsz, seq)
    for i in range(depth):
        w_nat, w_t = _in_proj_weights(w_in[i])
        flat = [xo.reshape(n_tok, d_model) for xo in x_orders]
        outs = _in_proj(flat, w_nat, w_t, b_gate[i][None, :], tables, seq)
        ka, qat, vat = outs[:3]
        gates = outs[-1]
        (oa,) = _attn(qat, ka, vat, a_sink[i], seq=seq, blocks_per_seq=seq // TOKEN_BLOCK,
                      n_q_heads=A_Q_HEADS, n_kv_heads=A_KV_HEADS, win=A_WINDOW, emit_lse=False,
                      name="attn_a")
        o_parts, lse_parts = [], []
        for g, (window, r) in enumerate(B_GROUPS):
            kg, qtg, vtg = outs[3 + 3 * g:6 + 3 * g]
            o, lse = _attn(qtg, kg, vtg, None, seq=seq,
                           blocks_per_seq=seq // r // TOKEN_BLOCK,
                           n_q_heads=B_HEADS_PER_GROUP, n_kv_heads=B_HEADS_PER_GROUP,
                           win=(window // 2) // r, emit_lse=True, name=f"attn_b{g}")
            o_parts.append(o)
            lse_parts.append(lse)
        weights = (w_branch_a[i].astype(bf16), w_branch_b[i].astype(bf16), w_out[i].astype(bf16),
                   ln1_g[i][None, :], ln1_b[i][None, :], w_up[i].astype(bf16),
                   w_down[i].astype(bf16), w_ple_gate[i].astype(bf16), b_ple_gate[i][None, :],
                   w_ple[i].astype(bf16), ln2_g[i][None, :], ln2_b[i][None, :])
        res = _post(x2d, oa, o_parts, lse_parts, gates, p_all, i, weights, alpha, bsz, seq,
                    emit_orders=i + 1 < depth)
        x2d, x_orders = res[0], res[1:]
    return x2d.reshape(bsz, seq, d_model)
```

```python
import functools
import math

import jax
import jax.numpy as jnp
import numpy as np
from jax import lax
from jax.experimental import pallas as pl
from jax.experimental.pallas import tpu as pltpu

D_MODEL = 1024
HEAD_DIM = 64
ROT_DIM = HEAD_DIM // 4
ROT_HALF = ROT_DIM // 2
ROPE_THETA = 500000.0

A_Q_HEADS = 8
A_KV_HEADS = 2
A_WINDOW = 128
B_GROUPS = ((128, 1), (512, 4), (2048, 16))
N_GROUPS = len(B_GROUPS)
B_HEADS_PER_GROUP = 4
DILATIONS = tuple(r for _, r in B_GROUPS)

A_Q_W = A_Q_HEADS * HEAD_DIM
A_KV_W = A_KV_HEADS * HEAD_DIM
B_GROUP_W = B_HEADS_PER_GROUP * HEAD_DIM
B_W = N_GROUPS * B_GROUP_W
GATE_W = 2 * D_MODEL
D_FF = 4 * D_MODEL
PLE_DIM = 256
LN_EPS = 1e-5

LANES = 128
SUBLANES = 8
TOKEN_BLOCK = 128
VMEM_LIMIT_BYTES = 56 * 1024 * 1024
NEG_BIG = -1e30
LOG2_E = math.log2(math.e)
LN_2 = math.log(2.0)

IN_PROJ_ROWS = 512
POST_ROWS = 256

_NAT_KA = 0
_NAT_KB = A_KV_W
_NAT_GATE = A_KV_W + B_W
NAT_W = _NAT_GATE + GATE_W
_T_QA = 0
_T_VA = A_Q_W
_T_B = A_Q_W + A_KV_W
T_W = _T_B + 2 * B_W


def _rope_tables(seq):
    pos = jnp.arange(seq, dtype=jnp.float32)
    inv_freq = ROPE_THETA ** (-jnp.arange(0, ROT_DIM, 2, dtype=jnp.float32) / ROT_DIM)
    ang = pos[:, None] * inv_freq[None, :]
    cos, sin = jnp.cos(ang), jnp.sin(ang)
    zeros = jnp.zeros((seq, HEAD_DIM - ROT_DIM), jnp.float32)
    zeros_h = jnp.zeros((seq, ROT_HALF), jnp.float32)
    reps = LANES // HEAD_DIM
    c = jnp.tile(jnp.concatenate([cos, cos, jnp.ones_like(zeros)], axis=-1), (1, reps))
    sa = jnp.tile(jnp.concatenate([-sin, zeros_h, zeros], axis=-1), (1, reps))
    sb = jnp.tile(jnp.concatenate([zeros_h, sin, zeros], axis=-1), (1, reps))

    def orders(t):
        return jnp.stack([t.reshape(seq // r, r, -1).transpose(1, 0, 2).reshape(seq, -1)
                          for r in DILATIONS])

    c, sa, sb, cos_o, sin_o = (orders(t) for t in (c, sa, sb, cos, sin))
    return c, sa, sb, cos_o.transpose(0, 2, 1), sin_o.transpose(0, 2, 1)


def _in_proj_weights(w):
    qa, ka, va, qb, kb, vb, gl = jnp.split(
        w, (A_Q_W, A_Q_W + A_KV_W, A_Q_W + 2 * A_KV_W, A_Q_W + 2 * A_KV_W + B_W,
            A_Q_W + 2 * A_KV_W + 2 * B_W, A_Q_W + 2 * A_KV_W + 3 * B_W), axis=1)
    scale = LOG2_E / math.sqrt(HEAD_DIM)
    w_nat = jnp.concatenate([ka, kb, gl], axis=1).astype(jnp.bfloat16)
    cols = [qa * scale, va]
    for g in range(N_GROUPS):
        cols += [qb[:, g * B_GROUP_W:(g + 1) * B_GROUP_W] * scale,
                 vb[:, g * B_GROUP_W:(g + 1) * B_GROUP_W]]
    w_t = jnp.concatenate(cols, axis=1).T.astype(jnp.bfloat16)
    return w_nat, w_t


def _store_class_major(dst_refs, slab_ref, val):
    rows, width = val.shape
    n_slabs = width // LANES
    for s in range(n_slabs):
        slab_ref[s] = val[:, s * LANES:(s + 1) * LANES]
    for r, dst_ref in zip(DILATIONS[1:], dst_refs):
        n = rows // r
        for c in range(r):
            dst_ref[c] = jnp.concatenate(
                [slab_ref[s, pl.ds(c, n, stride=r), :] for s in range(n_slabs)],
                axis=1).astype(dst_ref.dtype)


def _load_token_order(src_ref, slab_ref, r):
    n, width = src_ref.shape[1:]
    n_slabs = width // LANES
    for c in range(r):
        blk = src_ref[c].astype(jnp.float32)
        for s in range(n_slabs):
            slab_ref[s, pl.ds(c, n, stride=r), :] = blk[:, s * LANES:(s + 1) * LANES]
    return jnp.concatenate([slab_ref[s] for s in range(n_slabs)], axis=1)


def _reorder_kernel(x_ref, xb_ref, *rest):
    xp_refs, slab_ref = rest[:-1], rest[-1]
    x = x_ref[...]
    xb_ref[...] = x.astype(xb_ref.dtype)
    _store_class_major(xp_refs, slab_ref, x)


def _x_order_specs(bsz, seq, tm):
    per_seq = seq // tm
    specs = [pl.BlockSpec((tm, D_MODEL), lambda i: (i, 0))]
    shapes = [jax.ShapeDtypeStruct((bsz * seq, D_MODEL), jnp.bfloat16)]
    for r in DILATIONS[1:]:
        specs.append(pl.BlockSpec((None, r, tm // r, D_MODEL),
                                  lambda i: (i // per_seq, 0, i % per_seq, 0)))
        shapes.append(jax.ShapeDtypeStruct((bsz, r, seq // r, D_MODEL), jnp.bfloat16))
    return specs, shapes


def _slab_scratch(rows, width):
    return pltpu.VMEM((width // LANES, rows, LANES), jnp.float32)


def _reorder(x2d, bsz, seq):
    tm = POST_ROWS
    specs, shapes = _x_order_specs(bsz, seq, tm)
    return pl.pallas_call(
        _reorder_kernel,
        grid=(bsz * seq // tm,),
        in_specs=[pl.BlockSpec((tm, D_MODEL), lambda i: (i, 0))],
        out_specs=specs,
        out_shape=shapes,
        scratch_shapes=[_slab_scratch(tm, D_MODEL)],
        compiler_params=pltpu.CompilerParams(
            dimension_semantics=("arbitrary",), vmem_limit_bytes=VMEM_LIMIT_BYTES),
        name="reorder",
    )(x2d)


def _rope_lanes(t, c, sa, sb):
    up = pltpu.roll(t, LANES - ROT_HALF, axis=1)
    down = pltpu.roll(t, ROT_HALF, axis=1)
    return t * c + up * sa + down * sb


def _rope_sublanes(t, cos, sin, n_heads):
    parts = []
    for h in range(n_heads):
        a = t[h * HEAD_DIM:h * HEAD_DIM + ROT_HALF]
        b = t[h * HEAD_DIM + ROT_HALF:h * HEAD_DIM + ROT_DIM]
        parts += [a * cos - b * sin, b * cos + a * sin, t[h * HEAD_DIM + ROT_DIM:(h + 1) * HEAD_DIM]]
    return jnp.concatenate(parts, axis=0)


def _store_token_blocks(dst_ref, val):
    for j in range(val.shape[1] // LANES):
        dst_ref[j] = val[:, j * LANES:(j + 1) * LANES].astype(dst_ref.dtype)


def _in_proj_kernel(*refs):
    x_refs = refs[:N_GROUPS]
    wn_ref, wt_ref, bg_ref, cl_ref, sal_ref, sbl_ref, ct_ref, st_ref = refs[N_GROUPS:N_GROUPS + 8]
    outs = refs[N_GROUPS + 8:]
    ka_ref, qat_ref, vat_ref = outs[:3]
    b_refs = [outs[3 + 3 * g:6 + 3 * g] for g in range(N_GROUPS)]
    gate_ref = outs[3 + 3 * N_GROUPS]
    f32 = jnp.float32
    nt = (((1,), (1,)), ((), ()))

    def k_rope(acc, order):
        c, sa, sb = cl_ref[order], sal_ref[order], sbl_ref[order]
        return jnp.concatenate(
            [_rope_lanes(acc[:, t * LANES:(t + 1) * LANES], c, sa, sb)
             for t in range(acc.shape[1] // LANES)], axis=1)

    x0 = x_refs[0][...]
    acc = jnp.dot(x0, wn_ref[:, _NAT_KA:_NAT_KA + A_KV_W], preferred_element_type=f32)
    ka_ref[...] = k_rope(acc, 0).astype(ka_ref.dtype)
    acc = lax.dot_general(wt_ref[_T_QA:_T_QA + A_Q_W, :], x0, nt, preferred_element_type=f32)
    _store_token_blocks(qat_ref, _rope_sublanes(acc, ct_ref[0], st_ref[0], A_Q_HEADS))
    acc = lax.dot_general(wt_ref[_T_VA:_T_VA + A_KV_W, :], x0, nt, preferred_element_type=f32)
    _store_token_blocks(vat_ref, acc)
    for g in range(N_GROUPS):
        xg = x_refs[g][...]
        k_ref, qt_ref, vt_ref = b_refs[g]
        col = _NAT_KB + g * B_GROUP_W
        acc = jnp.dot(xg, wn_ref[:, col:col + B_GROUP_W], preferred_element_type=f32)
        k_ref[...] = k_rope(acc, g).astype(k_ref.dtype)
        row = _T_B + 2 * g * B_GROUP_W
        acc = lax.dot_general(wt_ref[row:row + B_GROUP_W, :], xg, nt, preferred_element_type=f32)
        _store_token_blocks(qt_ref, _rope_sublanes(acc, ct_ref[g], st_ref[g], B_HEADS_PER_GROUP))
        acc = lax.dot_general(wt_ref[row + B_GROUP_W:row + 2 * B_GROUP_W, :], xg, nt,
                              preferred_element_type=f32)
        _store_token_blocks(vt_ref, acc)
    chunk = 512
    for j in range(GATE_W // chunk):
        z = jnp.dot(x0, wn_ref[:, _NAT_GATE + j * chunk:_NAT_GATE + (j + 1) * chunk],
                    preferred_element_type=f32)
        z = z + bg_ref[:, j * chunk:(j + 1) * chunk]
        gate_ref[:, j * chunk:(j + 1) * chunk] = (1.0 / (1.0 + jnp.exp(-z))).astype(gate_ref.dtype)


def _in_proj(x_orders, w_nat, w_t, b_gate, tables, seq):
    tm = IN_PROJ_ROWS
    n_tok = x_orders[0].shape[0]
    per_seq = seq // tm
    blocks = tm // TOKEN_BLOCK
    const = lambda i: (0, 0)
    rows = lambda w: pl.BlockSpec((tm, w), lambda i: (i, 0))
    tblocks = lambda r: pl.BlockSpec((blocks, r, TOKEN_BLOCK), lambda i: (i, 0, 0))
    lane_tab = pl.BlockSpec((N_GROUPS, tm, LANES), lambda i: (0, i % per_seq, 0))
    sub_tab = pl.BlockSpec((N_GROUPS, ROT_HALF, tm), lambda i: (0, 0, i % per_seq))
    bf16 = jnp.bfloat16
    nat = lambda w: jax.ShapeDtypeStruct((n_tok, w), bf16)
    tsp = lambda r: jax.ShapeDtypeStruct((n_tok // TOKEN_BLOCK, r, TOKEN_BLOCK), bf16)
    out_specs = [rows(A_KV_W), tblocks(A_Q_W), tblocks(A_KV_W)]
    out_shape = [nat(A_KV_W), tsp(A_Q_W), tsp(A_KV_W)]
    for _ in range(N_GROUPS):
        out_specs += [rows(B_GROUP_W), tblocks(B_GROUP_W), tblocks(B_GROUP_W)]
        out_shape += [nat(B_GROUP_W), tsp(B_GROUP_W), tsp(B_GROUP_W)]
    out_specs.append(rows(GATE_W))
    out_shape.append(nat(GATE_W))
    return pl.pallas_call(
        _in_proj_kernel,
        grid=(n_tok // tm,),
        in_specs=[rows(D_MODEL)] * N_GROUPS + [
            pl.BlockSpec((D_MODEL, NAT_W), const, pipeline_mode=pl.Buffered(1)),
            pl.BlockSpec((T_W, D_MODEL), const, pipeline_mode=pl.Buffered(1)),
            pl.BlockSpec((1, GATE_W), const),
            lane_tab, lane_tab, lane_tab, sub_tab, sub_tab,
        ],
        out_specs=out_specs,
        out_shape=out_shape,
        compiler_params=pltpu.CompilerParams(
            dimension_semantics=("arbitrary",), vmem_limit_bytes=VMEM_LIMIT_BYTES),
        name="in_proj",
    )(*x_orders, w_nat, w_t, b_gate, *tables)


_BIAS_MASKED = 3


def _attn_bias(win):
    row = np.arange(TOKEN_BLOCK)[:, None]
    col = np.arange(2 * TOKEN_BLOCK)[None, :] % TOKEN_BLOCK
    tiles = [np.where(np.abs(col - row - t * TOKEN_BLOCK) <= win, 0.0, NEG_BIG)
             for t in (-1, 0, 1)]
    tiles.append(np.full((TOKEN_BLOCK, 2 * TOKEN_BLOCK), NEG_BIG))
    return jnp.asarray(np.stack(tiles), jnp.float32)


def _attn_kernel(*refs, blocks_per_seq, n_q_heads, n_kv_heads, win, has_sink, emit_lse):
    refs = list(refs)
    qt_ref, k_ref, vt_ref, bias_ref = refs[:4]
    pos = 4
    sink_ref = None
    if has_sink:
        sink_ref = refs[pos]
        pos += 1
    o_ref = refs[pos]
    pos += 1
    lse_ref = None
    if emit_lse:
        lse_ref = refs[pos]
        pos += 1
    s_bufs, m_bufs, p_bufs, d_bufs, l_bufs = (refs[pos + 2 * n:pos + 2 * n + 2] for n in range(5))

    n_blocks = qt_ref.shape[0]
    assert n_blocks % 2 == 0 and n_blocks >= 4
    grp = n_q_heads // n_kv_heads
    n_pairs = n_q_heads // 2
    tb = TOKEN_BLOCK
    offsets = (0,) if blocks_per_seq == 1 else (-1, 0, 1)
    f32, bf16 = jnp.float32, jnp.bfloat16
    lane = lax.broadcasted_iota(jnp.int32, (1, 2 * tb), 1)
    zeros = jnp.zeros((HEAD_DIM, tb), bf16)

    def heads_of(j):
        ha, hb = 2 * j, 2 * j + 1
        ka, kb = ha // grp, hb // grp
        assert ka // 2 == kb // 2
        return ha, hb, ka, kb

    def sink_of(j):
        ha, hb, _, _ = heads_of(j)
        return jnp.where(lane < tb, sink_ref[ha], sink_ref[hb]) * LOG2_E

    def tiles_of(i):
        local = i % blocks_per_seq
        out = {}
        for t in offsets:
            if t == 0:
                out[t] = (i, 1)
            else:
                ok = jnp.logical_and(local + t >= 0, local + t < blocks_per_seq)
                out[t] = (jnp.where(ok, i + t, i), jnp.where(ok, t + 1, _BIAS_MASKED))
        return out

    def scores(i, slot):
        tiles = tiles_of(i)
        for j in range(n_pairs):
            ha, hb, ka, kb = heads_of(j)
            qa = qt_ref[i, ha * HEAD_DIM:(ha + 1) * HEAD_DIM, :]
            qb = qt_ref[i, hb * HEAD_DIM:(hb + 1) * HEAD_DIM, :]
            halves = [jnp.concatenate([qa if ka % 2 == u else zeros,
                                       qb if kb % 2 == u else zeros], axis=1) for u in (0, 1)]
            rhs = jnp.concatenate(halves, axis=0)
            top = None
            for n, t in enumerate(offsets):
                blk, bias_idx = tiles[t]
                kt = k_ref[pl.ds(pl.multiple_of(blk * tb, tb), tb),
                           (ka // 2) * LANES:(ka // 2 + 1) * LANES]
                st = jnp.dot(kt, rhs, preferred_element_type=f32)
                if not (t == 0 and win >= tb - 1):
                    st = st + bias_ref[bias_idx]
                s_bufs[slot][j, n * tb:(n + 1) * tb, :] = st
                top = st if top is None else jnp.maximum(top, st)
            m = jnp.max(top, axis=0, keepdims=True)
            if has_sink:
                m = jnp.maximum(m, sink_of(j))
            m_bufs[slot][j] = jnp.broadcast_to(m, (SUBLANES, 2 * tb))

    def softmax(slot):
        for j in range(n_pairs):
            m = m_bufs[slot][j, 0:1, :]
            tot = None
            for n in range(len(offsets)):
                e = jnp.exp2(s_bufs[slot][j, n * tb:(n + 1) * tb, :] - m)
                p_bufs[slot][j, n * tb:(n + 1) * tb, :] = e.astype(bf16)
                tot = e if tot is None else tot + e
            denom = jnp.sum(tot, axis=0, keepdims=True)
            if has_sink:
                denom = denom + jnp.exp2(sink_of(j) - m)
            d_bufs[slot][j] = jnp.broadcast_to(denom, (SUBLANES, 2 * tb))
            if emit_lse:
                lse = (m + jnp.log2(denom)) * LN_2
                l_bufs[slot][j] = jnp.broadcast_to(lse, (SUBLANES, 2 * tb))

    def values(i, slot):
        tiles = tiles_of(i)
        q0 = pl.multiple_of(i * tb, tb)
        for j in range(n_pairs):
            _, _, ka, kb = heads_of(j)
            acc = None
            for n, t in enumerate(offsets):
                blk, _ = tiles[t]
                p = p_bufs[slot][j, n * tb:(n + 1) * tb, :]
                if ka == kb:
                    part = jnp.dot(vt_ref[blk, ka * HEAD_DIM:(ka + 1) * HEAD_DIM, :], p,
                                   preferred_element_type=f32)
                else:
                    part = jnp.concatenate(
                        [jnp.dot(vt_ref[blk, ka * HEAD_DIM:(ka + 1) * HEAD_DIM, :], p[:, :tb],
                                 preferred_element_type=f32),
                         jnp.dot(vt_ref[blk, kb * HEAD_DIM:(kb + 1) * HEAD_DIM, :], p[:, tb:],
                                 preferred_element_type=f32)], axis=1)
                acc = part if acc is None else acc + part
            ot = acc * (1.0 / d_bufs[slot][j, 0:1, :])
            pair = jnp.concatenate([ot[:, :tb], ot[:, tb:]], axis=0)
            o_ref[pl.ds(q0, tb), j * LANES:(j + 1) * LANES] = pair.T.astype(o_ref.dtype)
            if emit_lse:
                lse = l_bufs[slot][j, 0:1, :]
                both = jnp.concatenate([jnp.broadcast_to(lse[:, :tb], (HEAD_DIM, tb)),
                                        jnp.broadcast_to(lse[:, tb:], (HEAD_DIM, tb))], axis=0)
                lse_ref[pl.ds(q0, tb), j * LANES:(j + 1) * LANES] = both.T

    def step(i, slot):
        scores(i + 1, 1 - slot)
        values(i - 1, 1 - slot)
        softmax(slot)

    scores(0, 0)
    scores(1, 1)
    softmax(0)

    def body(k, carry):
        step(2 * k + 1, 1)
        step(2 * k + 2, 0)
        return carry

    lax.fori_loop(0, (n_blocks - 2) // 2, body, 0)
    values(n_blocks - 2, 0)
    softmax(1)
    values(n_blocks - 1, 1)


def _attn(qt, k, vt, sink, *, seq, blocks_per_seq, n_q_heads, n_kv_heads, win, emit_lse, name):
    n_tok = k.shape[0]
    blocks = seq // TOKEN_BLOCK
    qw, kw = n_q_heads * HEAD_DIM, n_kv_heads * HEAD_DIM
    n_pairs = n_q_heads // 2
    n_tiles = 1 if blocks_per_seq == 1 else 3
    kern = functools.partial(_attn_kernel, blocks_per_seq=blocks_per_seq, n_q_heads=n_q_heads,
                             n_kv_heads=n_kv_heads, win=win, has_sink=sink is not None,
                             emit_lse=emit_lse)
    in_specs = [pl.BlockSpec((blocks, qw, TOKEN_BLOCK), lambda b: (b, 0, 0)),
                pl.BlockSpec((seq, kw), lambda b: (b, 0)),
                pl.BlockSpec((blocks, kw, TOKEN_BLOCK), lambda b: (b, 0, 0)),
                pl.BlockSpec((_BIAS_MASKED + 1, TOKEN_BLOCK, 2 * TOKEN_BLOCK), lambda b: (0, 0, 0))]
    args = [qt, k, vt, _attn_bias(win)]
    if sink is not None:
        in_specs.append(pl.BlockSpec(memory_space=pltpu.SMEM))
        args.append(sink)
    out_specs = [pl.BlockSpec((seq, qw), lambda b: (b, 0))]
    out_shape = [jax.ShapeDtypeStruct((n_tok, qw), jnp.bfloat16)]
    if emit_lse:
        out_specs.append(pl.BlockSpec((seq, qw), lambda b: (b, 0)))
        out_shape.append(jax.ShapeDtypeStruct((n_tok, qw), jnp.float32))
    big = (n_pairs, n_tiles * TOKEN_BLOCK, 2 * TOKEN_BLOCK)
    small = (n_pairs, SUBLANES, 2 * TOKEN_BLOCK)
    scratch_shapes = ([pltpu.VMEM(big, jnp.float32)] * 2 + [pltpu.VMEM(small, jnp.float32)] * 2
                      + [pltpu.VMEM(big, jnp.bfloat16)] * 2 + [pltpu.VMEM(small, jnp.float32)] * 4)
    return pl.pallas_call(
        kern,
        grid=(n_tok // seq,),
        in_specs=in_specs,
        out_specs=out_specs,
        out_shape=out_shape,
        scratch_shapes=scratch_shapes,
        compiler_params=pltpu.CompilerParams(
            dimension_semantics=("arbitrary",), vmem_limit_bytes=VMEM_LIMIT_BYTES),
        name=name,
    )(*args)


def _layer_norm(y, g, b):
    mu = jnp.mean(y, axis=-1, keepdims=True)
    d = y - mu
    var = jnp.mean(d * d, axis=-1, keepdims=True)
    return d * lax.rsqrt(var + LN_EPS) * g + b


def _post_kernel(*refs, alpha, emit_orders):
    n_in = 2 + 2 * N_GROUPS + 2 + 12
    (x_ref, oa_ref) = refs[:2]
    o_refs = refs[2:2 + N_GROUPS]
    l_refs = refs[2 + N_GROUPS:2 + 2 * N_GROUPS]
    gate_ref, p_ref = refs[2 + 2 * N_GROUPS:4 + 2 * N_GROUPS]
    (wa_ref, wb_ref, wo_ref, g1_ref, b1_ref, wup_ref, wdn_ref, wpg_ref, bpg_ref, wple_ref,
     g2_ref, b2_ref) = refs[4 + 2 * N_GROUPS:n_in]
    out_ref = refs[n_in]
    n_order_outs = N_GROUPS if emit_orders else 0
    order_refs = refs[n_in + 1:n_in + 1 + n_order_outs]
    scratch = refs[n_in + 1 + n_order_outs:]
    f32, bf16 = jnp.float32, jnp.bfloat16

    os_, ls_ = [o_refs[0][...].astype(f32)], [l_refs[0][...]]
    for g in range(1, N_GROUPS):
        os_.append(_load_token_order(o_refs[g], scratch[2 * g - 1], DILATIONS[g]))
        ls_.append(_load_token_order(l_refs[g], scratch[2 * g], DILATIONS[g]))
    mx = functools.reduce(jnp.maximum, ls_)
    es = [jnp.exp(l - mx) for l in ls_]
    ob = functools.reduce(jnp.add, [e * o for e, o in zip(es, os_)]) / functools.reduce(jnp.add, es)
    ya = jnp.dot(oa_ref[...], wa_ref[...], preferred_element_type=f32)
    yb = jnp.dot(ob.astype(bf16), wb_ref[...], preferred_element_type=f32)
    ga = gate_ref[:, :D_MODEL].astype(f32)
    gb = gate_ref[:, D_MODEL:].astype(f32)
    merged = (ga * ya + gb * yb).astype(bf16)
    h = jnp.dot(merged, wo_ref[...], preferred_element_type=f32)
    x1 = _layer_norm(alpha * x_ref[...] + h, g1_ref[...], b1_ref[...])
    x1b = x1.astype(bf16)
    up = jnp.dot(x1b, wup_ref[...], preferred_element_type=f32)
    act = jnp.square(jnp.maximum(up, 0.0)).astype(bf16)
    mlp = jnp.dot(act, wdn_ref[...], preferred_element_type=f32)
    zg = jnp.dot(x1b, wpg_ref[...], preferred_element_type=f32) + bpg_ref[...]
    gate = 1.0 / (1.0 + jnp.exp(-zg))
    ple = gate * jnp.dot(p_ref[...].astype(bf16), wple_ref[...], preferred_element_type=f32)
    x2 = _layer_norm(alpha * x1 + mlp + ple, g2_ref[...], b2_ref[...])
    out_ref[...] = x2
    if emit_orders:
        order_refs[0][...] = x2.astype(bf16)
        _store_class_major(order_refs[1:], scratch[0], x2)


def _post(x2d, oa, o_parts, lse_parts, gates, p_all, layer, weights, alpha, bsz, seq, emit_orders):
    tm = POST_ROWS
    n_tok = x2d.shape[0]
    per_seq = seq // tm
    const = lambda i: (0, 0)
    rows = lambda w: pl.BlockSpec((tm, w), lambda i: (i, 0))
    resident = lambda shape: pl.BlockSpec(shape, const, pipeline_mode=pl.Buffered(1))
    vec = pl.BlockSpec((1, D_MODEL), const)

    def grouped(g):
        r = DILATIONS[g]
        if r == 1:
            return rows(B_GROUP_W)
        return pl.BlockSpec((None, r, tm // r, B_GROUP_W),
                            lambda i: (i // per_seq, 0, i % per_seq, 0))

    def as_grouped(a, g):
        r = DILATIONS[g]
        return a if r == 1 else a.reshape(bsz, r, seq // r, B_GROUP_W)

    out_specs = [rows(D_MODEL)]
    out_shape = [jax.ShapeDtypeStruct((n_tok, D_MODEL), jnp.float32)]
    if emit_orders:
        specs, shapes = _x_order_specs(bsz, seq, tm)
        out_specs += specs
        out_shape += shapes
    scratch_shapes = [_slab_scratch(tm, D_MODEL)]
    for g in range(1, N_GROUPS):
        scratch_shapes += [_slab_scratch(tm, B_GROUP_W)] * 2
    (wa, wb, wo, g1, b1, wup, wdn, wpg, bpg, wple, g2, b2) = weights
    return pl.pallas_call(
        functools.partial(_post_kernel, alpha=alpha, emit_orders=emit_orders),
        grid=(n_tok // tm,),
        in_specs=[rows(D_MODEL), rows(A_Q_W)]
        + [grouped(g) for g in range(N_GROUPS)] * 2
        + [rows(GATE_W), pl.BlockSpec((None, tm, PLE_DIM), lambda i: (layer, i, 0)),
           resident((A_Q_W, D_MODEL)), resident((B_GROUP_W, D_MODEL)),
           resident((D_MODEL, D_MODEL)), vec, vec,
           resident((D_MODEL, D_FF)), resident((D_FF, D_MODEL)),
           resident((D_MODEL, D_MODEL)), vec,
           resident((PLE_DIM, D_MODEL)), vec, vec],
        out_specs=out_specs,
        out_shape=out_shape,
        scratch_shapes=scratch_shapes,
        compiler_params=pltpu.CompilerParams(
            dimension_semantics=("arbitrary",), vmem_limit_bytes=VMEM_LIMIT_BYTES),
        name="post",
    )(x2d, oa, *[as_grouped(o, g) for g, o in enumerate(o_parts)],
      *[as_grouped(l, g) for g, l in enumerate(lse_parts)], gates, p_all,
      wa, wb, wo, g1, b1, wup, wdn, wpg, bpg, wple, g2, b2)


def kernel(x, p, w_in, b_gate, a_sink, w_branch_a, w_branch_b, w_out, ln1_g, ln1_b, w_up, w_down,
           w_ple_gate, b_ple_gate, w_ple, ln2_g, ln2_b):
    bsz, seq, d_model = x.shape
    depth = w_in.shape[0]
    assert d_model == D_MODEL and w_in.shape[2] == NAT_W + T_W
    assert seq % (TOKEN_BLOCK * max(DILATIONS)) == 0 and seq % IN_PROJ_ROWS == 0
    assert POST_ROWS % (2 * SUBLANES * max(DILATIONS)) == 0
    n_tok = bsz * seq
    alpha = (2 * depth) ** 0.25
    bf16 = jnp.bfloat16
    tables = _rope_tables(seq)
    p_all = p.reshape(depth, n_tok, PLE_DIM)
    x2d = x.reshape(n_tok, d_model)
    x_orders = _reorder(x2d, bsz, seq)
    for i in range(depth):
        w_nat, w_t = _in_proj_weights(w_in[i])
        flat = [xo.reshape(n_tok, d_model) for xo in x_orders]
        outs = _in_proj(flat, w_nat, w_t, b_gate[i][None, :], tables, seq)
        ka, qat, vat = outs[:3]
        gates = outs[-1]
        (oa,) = _attn(qat, ka, vat, a_sink[i], seq=seq, blocks_per_seq=seq // TOKEN_BLOCK,
                      n_q_heads=A_Q_HEADS, n_kv_heads=A_KV_HEADS, win=A_WINDOW, emit_lse=False,
                      name="attn_a")
        o_parts, lse_parts = [], []
        for g, (window, r) in enumerate(B_GROUPS):
            kg, qtg, vtg = outs[3 + 3 * g:6 + 3 * g]
            o, lse = _attn(qtg, kg, vtg, None, seq=seq,
                           blocks_per_seq=seq // r // TOKEN_BLOCK,
                           n_q_heads=B_HEADS_PER_GROUP, n_kv_heads=B_HEADS_PER_GROUP,
                           win=(window // 2) // r, emit_lse=True, name=f"attn_b{g}")
            o_parts.append(o)
            lse_parts.append(lse)
        weights = (w_branch_a[i].astype(bf16), w_branch_b[i].astype(bf16), w_out[i].astype(bf16),
                   ln1_g[i][None, :], ln1_b[i][None, :], w_up[i].astype(bf16),
                   w_down[i].astype(bf16), w_ple_gate[i].astype(bf16), b_ple_gate[i][None, :],
                   w_ple[i].astype(bf16), ln2_g[i][None, :], ln2_b[i][None, :])
        res = _post(x2d, oa, o_parts, lse_parts, gates, p_all, i, weights, alpha, bsz, seq,
                    emit_orders=i + 1 < depth)
        x2d, x_orders = res[0], res[1:]
    return x2d.reshape(bsz, seq, d_model)
```

```python
import functools
import math

import jax
import jax.numpy as jnp
import numpy as np
from jax import lax
from jax.experimental import pallas as pl
from jax.experimental.pallas import tpu as pltpu

D_MODEL = 1024
HEAD_DIM = 64
ROT_DIM = HEAD_DIM // 4
ROT_HALF = ROT_DIM // 2
ROPE_THETA = 500000.0

A_Q_HEADS = 8
A_KV_HEADS = 2
A_WINDOW = 128
B_GROUPS = ((128, 1), (512, 4), (2048, 16))
N_GROUPS = len(B_GROUPS)
B_HEADS_PER_GROUP = 4
DILATIONS = tuple(r for _, r in B_GROUPS)

A_Q_W = A_Q_HEADS * HEAD_DIM
A_KV_W = A_KV_HEADS * HEAD_DIM
B_GROUP_W = B_HEADS_PER_GROUP * HEAD_DIM
B_W = N_GROUPS * B_GROUP_W
GATE_W = 2 * D_MODEL
D_FF = 4 * D_MODEL
PLE_DIM = 256
LN_EPS = 1e-5

LANES = 128
SUBLANES = 8
TOKEN_BLOCK = 128
VMEM_LIMIT_BYTES = 56 * 1024 * 1024
NEG_BIG = -1e30
LOG2_E = math.log2(math.e)
LN_2 = math.log(2.0)

IN_PROJ_ROWS = 512
POST_ROWS = 256

_NAT_KA = 0
_NAT_KB = A_KV_W
_NAT_GATE = A_KV_W + B_W
NAT_W = _NAT_GATE + GATE_W
_T_QA = 0
_T_VA = A_Q_W
_T_B = A_Q_W + A_KV_W
T_W = _T_B + 2 * B_W


def _rope_tables(seq):
    pos = jnp.arange(seq, dtype=jnp.float32)
    inv_freq = ROPE_THETA ** (-jnp.arange(0, ROT_DIM, 2, dtype=jnp.float32) / ROT_DIM)
    ang = pos[:, None] * inv_freq[None, :]
    cos, sin = jnp.cos(ang), jnp.sin(ang)
    zeros = jnp.zeros((seq, HEAD_DIM - ROT_DIM), jnp.float32)
    zeros_h = jnp.zeros((seq, ROT_HALF), jnp.float32)
    reps = LANES // HEAD_DIM
    c = jnp.tile(jnp.concatenate([cos, cos, jnp.ones_like(zeros)], axis=-1), (1, reps))
    sa = jnp.tile(jnp.concatenate([-sin, zeros_h, zeros], axis=-1), (1, reps))
    sb = jnp.tile(jnp.concatenate([zeros_h, sin, zeros], axis=-1), (1, reps))

    def orders(t):
        return jnp.stack([t.reshape(seq // r, r, -1).transpose(1, 0, 2).reshape(seq, -1)
                          for r in DILATIONS])

    c, sa, sb, cos_o, sin_o = (orders(t) for t in (c, sa, sb, cos, sin))
    return c, sa, sb, cos_o.transpose(0, 2, 1), sin_o.transpose(0, 2, 1)


def _in_proj_weights(w):
    qa, ka, va, qb, kb, vb, gl = jnp.split(
        w, (A_Q_W, A_Q_W + A_KV_W, A_Q_W + 2 * A_KV_W, A_Q_W + 2 * A_KV_W + B_W,
            A_Q_W + 2 * A_KV_W + 2 * B_W, A_Q_W + 2 * A_KV_W + 3 * B_W), axis=1)
    scale = LOG2_E / math.sqrt(HEAD_DIM)
    w_nat = jnp.concatenate([ka, kb, gl], axis=1).astype(jnp.bfloat16)
    cols = [qa * scale, va]
    for g in range(N_GROUPS):
        cols += [qb[:, g * B_GROUP_W:(g + 1) * B_GROUP_W] * scale,
                 vb[:, g * B_GROUP_W:(g + 1) * B_GROUP_W]]
    w_t = jnp.concatenate(cols, axis=1).T.astype(jnp.bfloat16)
    return w_nat, w_t


def _fill_slabs(slab_ref, rows, val):
    for s in range(slab_ref.shape[0]):
        slab_ref[s, rows, :] = val[:, s * LANES:(s + 1) * LANES]


def _read_slabs(slab_ref, rows):
    return jnp.concatenate([slab_ref[s, rows, :] for s in range(slab_ref.shape[0])], axis=1)


def _slabs_to_class_major(dst_refs, slab_ref):
    n_slabs, rows, _ = slab_ref.shape
    for r, dst_ref in zip(DILATIONS[1:], dst_refs):
        n = rows // r
        for c in range(r):
            dst_ref[c] = jnp.concatenate(
                [slab_ref[s, pl.ds(c, n, stride=r), :] for s in range(n_slabs)],
                axis=1).astype(dst_ref.dtype)


def _scatter_token_order(src_ref, slab_ref, r):
    n = src_ref.shape[1]
    for c in range(r):
        blk = src_ref[c].astype(jnp.float32)
        for s in range(slab_ref.shape[0]):
            slab_ref[s, pl.ds(c, n, stride=r), :] = blk[:, s * LANES:(s + 1) * LANES]


def _reorder_kernel(x_ref, xb_ref, *rest):
    xp_refs, slab_ref = rest[:-1], rest[-1]
    x = x_ref[...]
    xb_ref[...] = x.astype(xb_ref.dtype)
    _fill_slabs(slab_ref, slice(None), x)
    _slabs_to_class_major(xp_refs, slab_ref)


def _x_order_specs(bsz, seq, tm):
    per_seq = seq // tm
    specs = [pl.BlockSpec((tm, D_MODEL), lambda i: (i, 0))]
    shapes = [jax.ShapeDtypeStruct((bsz * seq, D_MODEL), jnp.bfloat16)]
    for r in DILATIONS[1:]:
        specs.append(pl.BlockSpec((None, r, tm // r, D_MODEL),
                                  lambda i: (i // per_seq, 0, i % per_seq, 0)))
        shapes.append(jax.ShapeDtypeStruct((bsz, r, seq // r, D_MODEL), jnp.bfloat16))
    return specs, shapes


def _slab_scratch(rows, width):
    return pltpu.VMEM((width // LANES, rows, LANES), jnp.float32)


def _reorder(x2d, bsz, seq):
    tm = POST_ROWS
    specs, shapes = _x_order_specs(bsz, seq, tm)
    return pl.pallas_call(
        _reorder_kernel,
        grid=(bsz * seq // tm,),
        in_specs=[pl.BlockSpec((tm, D_MODEL), lambda i: (i, 0))],
        out_specs=specs,
        out_shape=shapes,
        scratch_shapes=[_slab_scratch(tm, D_MODEL)],
        compiler_params=pltpu.CompilerParams(
            dimension_semantics=("arbitrary",), vmem_limit_bytes=VMEM_LIMIT_BYTES),
        name="reorder",
    )(x2d)


def _rope_lanes(t, c, sa, sb):
    up = pltpu.roll(t, LANES - ROT_HALF, axis=1)
    down = pltpu.roll(t, ROT_HALF, axis=1)
    return t * c + up * sa + down * sb


def _rope_sublanes(t, cos, sin, n_heads):
    parts = []
    for h in range(n_heads):
        a = t[h * HEAD_DIM:h * HEAD_DIM + ROT_HALF]
        b = t[h * HEAD_DIM + ROT_HALF:h * HEAD_DIM + ROT_DIM]
        parts += [a * cos - b * sin, b * cos + a * sin, t[h * HEAD_DIM + ROT_DIM:(h + 1) * HEAD_DIM]]
    return jnp.concatenate(parts, axis=0)


def _store_token_blocks(dst_ref, val):
    for j in range(val.shape[1] // LANES):
        dst_ref[j] = val[:, j * LANES:(j + 1) * LANES].astype(dst_ref.dtype)


def _in_proj_kernel(*refs):
    x_refs = refs[:N_GROUPS]
    wn_ref, wt_ref, bg_ref, cl_ref, sal_ref, sbl_ref, ct_ref, st_ref = refs[N_GROUPS:N_GROUPS + 8]
    outs = refs[N_GROUPS + 8:]
    ka_ref, qat_ref, vat_ref = outs[:3]
    b_refs = [outs[3 + 3 * g:6 + 3 * g] for g in range(N_GROUPS)]
    gate_ref = outs[3 + 3 * N_GROUPS]
    f32 = jnp.float32
    nt = (((1,), (1,)), ((), ()))

    def k_rope(acc, order):
        c, sa, sb = cl_ref[order], sal_ref[order], sbl_ref[order]
        return jnp.concatenate(
            [_rope_lanes(acc[:, t * LANES:(t + 1) * LANES], c, sa, sb)
             for t in range(acc.shape[1] // LANES)], axis=1)

    x0 = x_refs[0][...]
    acc = jnp.dot(x0, wn_ref[:, _NAT_KA:_NAT_KA + A_KV_W], preferred_element_type=f32)
    ka_ref[...] = k_rope(acc, 0).astype(ka_ref.dtype)
    acc = lax.dot_general(wt_ref[_T_QA:_T_QA + A_Q_W, :], x0, nt, preferred_element_type=f32)
    _store_token_blocks(qat_ref, _rope_sublanes(acc, ct_ref[0], st_ref[0], A_Q_HEADS))
    acc = lax.dot_general(wt_ref[_T_VA:_T_VA + A_KV_W, :], x0, nt, preferred_element_type=f32)
    _store_token_blocks(vat_ref, acc)
    for g in range(N_GROUPS):
        xg = x_refs[g][...]
        k_ref, qt_ref, vt_ref = b_refs[g]
        col = _NAT_KB + g * B_GROUP_W
        acc = jnp.dot(xg, wn_ref[:, col:col + B_GROUP_W], preferred_element_type=f32)
        k_ref[...] = k_rope(acc, g).astype(k_ref.dtype)
        row = _T_B + 2 * g * B_GROUP_W
        acc = lax.dot_general(wt_ref[row:row + B_GROUP_W, :], xg, nt, preferred_element_type=f32)
        _store_token_blocks(qt_ref, _rope_sublanes(acc, ct_ref[g], st_ref[g], B_HEADS_PER_GROUP))
        acc = lax.dot_general(wt_ref[row + B_GROUP_W:row + 2 * B_GROUP_W, :], xg, nt,
                              preferred_element_type=f32)
        _store_token_blocks(vt_ref, acc)
    chunk = 512
    for j in range(GATE_W // chunk):
        z = jnp.dot(x0, wn_ref[:, _NAT_GATE + j * chunk:_NAT_GATE + (j + 1) * chunk],
                    preferred_element_type=f32)
        z = z + bg_ref[:, j * chunk:(j + 1) * chunk]
        gate_ref[:, j * chunk:(j + 1) * chunk] = (1.0 / (1.0 + jnp.exp(-z))).astype(gate_ref.dtype)


def _in_proj(x_orders, w_nat, w_t, b_gate, tables, seq):
    tm = IN_PROJ_ROWS
    n_tok = x_orders[0].shape[0]
    per_seq = seq // tm
    blocks = tm // TOKEN_BLOCK
    const = lambda i: (0, 0)
    rows = lambda w: pl.BlockSpec((tm, w), lambda i: (i, 0))
    tblocks = lambda r: pl.BlockSpec((blocks, r, TOKEN_BLOCK), lambda i: (i, 0, 0))
    lane_tab = pl.BlockSpec((N_GROUPS, tm, LANES), lambda i: (0, i % per_seq, 0))
    sub_tab = pl.BlockSpec((N_GROUPS, ROT_HALF, tm), lambda i: (0, 0, i % per_seq))
    bf16 = jnp.bfloat16
    nat = lambda w: jax.ShapeDtypeStruct((n_tok, w), bf16)
    tsp = lambda r: jax.ShapeDtypeStruct((n_tok // TOKEN_BLOCK, r, TOKEN_BLOCK), bf16)
    out_specs = [rows(A_KV_W), tblocks(A_Q_W), tblocks(A_KV_W)]
    out_shape = [nat(A_KV_W), tsp(A_Q_W), tsp(A_KV_W)]
    for _ in range(N_GROUPS):
        out_specs += [rows(B_GROUP_W), tblocks(B_GROUP_W), tblocks(B_GROUP_W)]
        out_shape += [nat(B_GROUP_W), tsp(B_GROUP_W), tsp(B_GROUP_W)]
    out_specs.append(rows(GATE_W))
    out_shape.append(nat(GATE_W))
    return pl.pallas_call(
        _in_proj_kernel,
        grid=(n_tok // tm,),
        in_specs=[rows(D_MODEL)] * N_GROUPS + [
            pl.BlockSpec((D_MODEL, NAT_W), const, pipeline_mode=pl.Buffered(1)),
            pl.BlockSpec((T_W, D_MODEL), const, pipeline_mode=pl.Buffered(1)),
            pl.BlockSpec((1, GATE_W), const),
            lane_tab, lane_tab, lane_tab, sub_tab, sub_tab,
        ],
        out_specs=out_specs,
        out_shape=out_shape,
        compiler_params=pltpu.CompilerParams(
            dimension_semantics=("arbitrary",), vmem_limit_bytes=VMEM_LIMIT_BYTES),
        name="in_proj",
    )(*x_orders, w_nat, w_t, b_gate, *tables)


_BIAS_MASKED = 3
ROUND_BLOCKS = 2


def _attn_bias(win):
    row = np.arange(TOKEN_BLOCK)[:, None]
    col = np.arange(2 * TOKEN_BLOCK)[None, :] % TOKEN_BLOCK
    tiles = [np.where(np.abs(col - row - t * TOKEN_BLOCK) <= win, 0.0, NEG_BIG)
             for t in (-1, 0, 1)]
    tiles.append(np.full((TOKEN_BLOCK, 2 * TOKEN_BLOCK), NEG_BIG))
    return jnp.asarray(np.stack(tiles), jnp.float32)


def _round_robin(streams):
    streams = list(streams)
    while streams:
        for s in list(streams):
            if next(s, StopIteration) is StopIteration:
                streams.remove(s)


def _fold_rows(op, tile, acc):
    for r in range(tile.shape[0] // SUBLANES):
        part = tile[r * SUBLANES:(r + 1) * SUBLANES, :]
        acc = part if acc is None else op(acc, part)
    return acc


def _attn_kernel(*refs, blocks_per_seq, n_q_heads, n_kv_heads, win, has_sink, emit_lse):
    refs = list(refs)
    qt_ref, k_ref, vt_ref, bias_ref = refs[:4]
    pos = 4
    sink_ref = None
    if has_sink:
        sink_ref = refs[pos]
        pos += 1
    o_ref = refs[pos]
    pos += 1
    lse_ref = None
    if emit_lse:
        lse_ref = refs[pos]
        pos += 1
    n_slots = 2 * ROUND_BLOCKS
    s_bufs, m_bufs, p_bufs, d_bufs, l_bufs = (
        refs[pos + n_slots * n:pos + n_slots * (n + 1)] for n in range(5))

    n_blocks = qt_ref.shape[0]
    n_rounds = n_blocks // ROUND_BLOCKS
    assert n_blocks % ROUND_BLOCKS == 0 and n_rounds % 2 == 0 and n_rounds >= 4
    grp = n_q_heads // n_kv_heads
    n_pairs = n_q_heads // 2
    tb = TOKEN_BLOCK
    offsets = (0,) if blocks_per_seq == 1 else (-1, 0, 1)
    f32, bf16 = jnp.float32, jnp.bfloat16
    lane = lax.broadcasted_iota(jnp.int32, (1, 2 * tb), 1)
    zeros = jnp.zeros((HEAD_DIM, tb), bf16)

    def heads_of(j):
        ha, hb = 2 * j, 2 * j + 1
        ka, kb = ha // grp, hb // grp
        assert ka // 2 == kb // 2
        return ha, hb, ka, kb

    def sink_of(j):
        ha, hb, _, _ = heads_of(j)
        return jnp.where(lane < tb, sink_ref[ha], sink_ref[hb]) * LOG2_E

    def tiles_of(i):
        local = i % blocks_per_seq
        out = {}
        for t in offsets:
            if t == 0:
                out[t] = (i, 1)
            else:
                ok = jnp.logical_and(local + t >= 0, local + t < blocks_per_seq)
                out[t] = (jnp.where(ok, i + t, i), jnp.where(ok, t + 1, _BIAS_MASKED))
        return out

    def scores(i, slot, pairs):
        tiles = tiles_of(i)
        for j in pairs:
            ha, hb, ka, kb = heads_of(j)
            qa = qt_ref[i, ha * HEAD_DIM:(ha + 1) * HEAD_DIM, :]
            qb = qt_ref[i, hb * HEAD_DIM:(hb + 1) * HEAD_DIM, :]
            halves = [jnp.concatenate([qa if ka % 2 == u else zeros,
                                       qb if kb % 2 == u else zeros], axis=1) for u in (0, 1)]
            rhs = jnp.concatenate(halves, axis=0)
            top = None
            for n, t in enumerate(offsets):
                blk, bias_idx = tiles[t]
                kt = k_ref[pl.ds(pl.multiple_of(blk * tb, tb), tb),
                           (ka // 2) * LANES:(ka // 2 + 1) * LANES]
                st = jnp.dot(kt, rhs, preferred_element_type=f32)
                if not (t == 0 and win >= tb - 1):
                    st = st + bias_ref[bias_idx]
                s_bufs[slot][j, n * tb:(n + 1) * tb, :] = st
                top = _fold_rows(jnp.maximum, st, top)
                yield
            m = jnp.max(top, axis=0, keepdims=True)
            if has_sink:
                m = jnp.maximum(m, sink_of(j))
            m_bufs[slot][j] = jnp.broadcast_to(m, (SUBLANES, 2 * tb))

    def softmax(slot, pairs):
        for j in pairs:
            m = m_bufs[slot][j, 0:1, :]
            tot = None
            for n in range(len(offsets)):
                e = jnp.exp2(s_bufs[slot][j, n * tb:(n + 1) * tb, :] - m)
                p_bufs[slot][j, n * tb:(n + 1) * tb, :] = e.astype(bf16)
                tot = _fold_rows(jnp.add, e, tot)
                yield
            denom = jnp.sum(tot, axis=0, keepdims=True)
            if has_sink:
                denom = denom + jnp.exp2(sink_of(j) - m)
            d_bufs[slot][j] = jnp.broadcast_to(denom, (SUBLANES, 2 * tb))
            if emit_lse:
                lse = (m + jnp.log2(denom)) * LN_2
                l_bufs[slot][j] = jnp.broadcast_to(lse, (SUBLANES, 2 * tb))

    def values(i, slot, pairs):
        tiles = tiles_of(i)
        q0 = pl.multiple_of(i * tb, tb)
        for j in pairs:
            _, _, ka, kb = heads_of(j)
            acc = None
            for n, t in enumerate(offsets):
                blk, _ = tiles[t]
                p = p_bufs[slot][j, n * tb:(n + 1) * tb, :]
                if ka == kb:
                    part = jnp.dot(vt_ref[blk, ka * HEAD_DIM:(ka + 1) * HEAD_DIM, :], p,
                                   preferred_element_type=f32)
                else:
                    part = jnp.concatenate(
                        [jnp.dot(vt_ref[blk, ka * HEAD_DIM:(ka + 1) * HEAD_DIM, :], p[:, :tb],
                                 preferred_element_type=f32),
                         jnp.dot(vt_ref[blk, kb * HEAD_DIM:(kb + 1) * HEAD_DIM, :], p[:, tb:],
                                 preferred_element_type=f32)], axis=1)
                acc = part if acc is None else acc + part
                yield
            ot = acc * (1.0 / d_bufs[slot][j, 0:1, :])
            pair = jnp.concatenate([ot[:, :tb], ot[:, tb:]], axis=0)
            o_ref[pl.ds(q0, tb), j * LANES:(j + 1) * LANES] = pair.T.astype(o_ref.dtype)
            if emit_lse:
                lse = l_bufs[slot][j, 0:1, :]
                both = jnp.concatenate([jnp.broadcast_to(lse[:, :tb], (HEAD_DIM, tb)),
                                        jnp.broadcast_to(lse[:, tb:], (HEAD_DIM, tb))], axis=0)
                lse_ref[pl.ds(q0, tb), j * LANES:(j + 1) * LANES] = both.T

    def do_round(r, parity, first=False, last=False):
        cur = [parity * ROUND_BLOCKS + b for b in range(ROUND_BLOCKS)]
        oth = [(1 - parity) * ROUND_BLOCKS + b for b in range(ROUND_BLOCKS)]
        for j in range(n_pairs):
            for b in range(ROUND_BLOCKS):
                streams = [softmax(cur[b], [j])]
                if not first:
                    streams.append(values((r - 1) * ROUND_BLOCKS + b, oth[b], [j]))
                if not last:
                    streams.append(scores((r + 1) * ROUND_BLOCKS + b, oth[b], [j]))
                _round_robin(streams)

    all_pairs = range(n_pairs)
    for b in range(ROUND_BLOCKS):
        _round_robin([scores(b, b, all_pairs)])
    do_round(0, 0, first=True)

    def body(k, carry):
        do_round(2 * k + 1, 1)
        do_round(2 * k + 2, 0)
        return carry

    lax.fori_loop(0, (n_rounds - 2) // 2, body, 0)
    do_round(n_rounds - 1, 1, last=True)
    for b in range(ROUND_BLOCKS):
        _round_robin([values((n_rounds - 1) * ROUND_BLOCKS + b, ROUND_BLOCKS + b, all_pairs)])


def _attn(qt, k, vt, sink, *, seq, blocks_per_seq, n_q_heads, n_kv_heads, win, emit_lse, name):
    n_tok = k.shape[0]
    blocks = seq // TOKEN_BLOCK
    qw, kw = n_q_heads * HEAD_DIM, n_kv_heads * HEAD_DIM
    n_pairs = n_q_heads // 2
    n_tiles = 1 if blocks_per_seq == 1 else 3
    kern = functools.partial(_attn_kernel, blocks_per_seq=blocks_per_seq, n_q_heads=n_q_heads,
                             n_kv_heads=n_kv_heads, win=win, has_sink=sink is not None,
                             emit_lse=emit_lse)
    in_specs = [pl.BlockSpec((blocks, qw, TOKEN_BLOCK), lambda b: (b, 0, 0)),
                pl.BlockSpec((seq, kw), lambda b: (b, 0)),
                pl.BlockSpec((blocks, kw, TOKEN_BLOCK), lambda b: (b, 0, 0)),
                pl.BlockSpec((_BIAS_MASKED + 1, TOKEN_BLOCK, 2 * TOKEN_BLOCK), lambda b: (0, 0, 0))]
    args = [qt, k, vt, _attn_bias(win)]
    if sink is not None:
        in_specs.append(pl.BlockSpec(memory_space=pltpu.SMEM))
        args.append(sink)
    out_specs = [pl.BlockSpec((seq, qw), lambda b: (b, 0))]
    out_shape = [jax.ShapeDtypeStruct((n_tok, qw), jnp.bfloat16)]
    if emit_lse:
        out_specs.append(pl.BlockSpec((seq, qw), lambda b: (b, 0)))
        out_shape.append(jax.ShapeDtypeStruct((n_tok, qw), jnp.float32))
    big = (n_pairs, n_tiles * TOKEN_BLOCK, 2 * TOKEN_BLOCK)
    small = (n_pairs, SUBLANES, 2 * TOKEN_BLOCK)
    n_slots = 2 * ROUND_BLOCKS
    scratch_shapes = ([pltpu.VMEM(big, jnp.float32)] * n_slots
                      + [pltpu.VMEM(small, jnp.float32)] * n_slots
                      + [pltpu.VMEM(big, jnp.bfloat16)] * n_slots
                      + [pltpu.VMEM(small, jnp.float32)] * (2 * n_slots))
    return pl.pallas_call(
        kern,
        grid=(n_tok // seq,),
        in_specs=in_specs,
        out_specs=out_specs,
        out_shape=out_shape,
        scratch_shapes=scratch_shapes,
        compiler_params=pltpu.CompilerParams(
            dimension_semantics=("arbitrary",), vmem_limit_bytes=VMEM_LIMIT_BYTES),
        name=name,
    )(*args)


def _layer_norm(y, g, b):
    mu = jnp.mean(y, axis=-1, keepdims=True)
    d = y - mu
    var = jnp.mean(d * d, axis=-1, keepdims=True)
    return d * lax.rsqrt(var + LN_EPS) * g + b


def _post_kernel(*refs, alpha, emit_orders):
    n_in = 2 + 2 * N_GROUPS + 2 + 12
    (x_ref, oa_ref) = refs[:2]
    o_refs = refs[2:2 + N_GROUPS]
    l_refs = refs[2 + N_GROUPS:2 + 2 * N_GROUPS]
    gate_ref, p_ref = refs[2 + 2 * N_GROUPS:4 + 2 * N_GROUPS]
    (wa_ref, wb_ref, wo_ref, g1_ref, b1_ref, wup_ref, wdn_ref, wpg_ref, bpg_ref, wple_ref,
     g2_ref, b2_ref) = refs[4 + 2 * N_GROUPS:n_in]
    out_ref = refs[n_in]
    n_order_outs = N_GROUPS if emit_orders else 0
    order_refs = refs[n_in + 1:n_in + 1 + n_order_outs]
    slabs = refs[n_in + 1 + n_order_outs:]
    f32, bf16 = jnp.float32, jnp.bfloat16
    everything = slice(None)

    os_, ls_ = [o_refs[0][...].astype(f32)], [l_refs[0][...]]
    for g in range(1, N_GROUPS):
        _scatter_token_order(o_refs[g], slabs[2 * g - 1], DILATIONS[g])
        _scatter_token_order(l_refs[g], slabs[2 * g], DILATIONS[g])
        os_.append(_read_slabs(slabs[2 * g - 1], everything))
        ls_.append(_read_slabs(slabs[2 * g], everything))
    mx = functools.reduce(jnp.maximum, ls_)
    es = [jnp.exp(l - mx) for l in ls_]
    ob = functools.reduce(jnp.add, [e * o for e, o in zip(es, os_)]) / functools.reduce(jnp.add, es)
    ya = jnp.dot(oa_ref[...], wa_ref[...], preferred_element_type=f32)
    yb = jnp.dot(ob.astype(bf16), wb_ref[...], preferred_element_type=f32)
    ga = gate_ref[:, :D_MODEL].astype(f32)
    gb = gate_ref[:, D_MODEL:].astype(f32)
    merged = (ga * ya + gb * yb).astype(bf16)
    h = jnp.dot(merged, wo_ref[...], preferred_element_type=f32)
    x1 = _layer_norm(alpha * x_ref[...] + h, g1_ref[...], b1_ref[...])
    x1b = x1.astype(bf16)
    up = jnp.dot(x1b, wup_ref[...], preferred_element_type=f32)
    act = jnp.square(jnp.maximum(up, 0.0)).astype(bf16)
    mlp = jnp.dot(act, wdn_ref[...], preferred_element_type=f32)
    zg = jnp.dot(x1b, wpg_ref[...], preferred_element_type=f32) + bpg_ref[...]
    gate = 1.0 / (1.0 + jnp.exp(-zg))
    ple = gate * jnp.dot(p_ref[...].astype(bf16), wple_ref[...], preferred_element_type=f32)
    x2 = _layer_norm(alpha * x1 + mlp + ple, g2_ref[...], b2_ref[...])
    out_ref[...] = x2
    if emit_orders:
        order_refs[0][...] = x2.astype(bf16)
        _fill_slabs(slabs[0], everything, x2)
        _slabs_to_class_major(order_refs[1:], slabs[0])


def _post(x2d, oa, o_parts, lse_parts, gates, p_all, layer, weights, alpha, bsz, seq, emit_orders):
    tm = POST_ROWS
    n_tok = x2d.shape[0]
    per_seq = seq // tm
    const = lambda i: (0, 0)
    rows = lambda w: pl.BlockSpec((tm, w), lambda i: (i, 0))
    resident = lambda shape: pl.BlockSpec(shape, const, pipeline_mode=pl.Buffered(1))
    vec = pl.BlockSpec((1, D_MODEL), const)

    def grouped(g):
        r = DILATIONS[g]
        if r == 1:
            return rows(B_GROUP_W)
        return pl.BlockSpec((None, r, tm // r, B_GROUP_W),
                            lambda i: (i // per_seq, 0, i % per_seq, 0))

    def as_grouped(a, g):
        r = DILATIONS[g]
        return a if r == 1 else a.reshape(bsz, r, seq // r, B_GROUP_W)

    out_specs = [rows(D_MODEL)]
    out_shape = [jax.ShapeDtypeStruct((n_tok, D_MODEL), jnp.float32)]
    if emit_orders:
        specs, shapes = _x_order_specs(bsz, seq, tm)
        out_specs += specs
        out_shape += shapes
    scratch_shapes = [_slab_scratch(tm, D_MODEL)]
    for g in range(1, N_GROUPS):
        scratch_shapes += [_slab_scratch(tm, B_GROUP_W)] * 2
    (wa, wb, wo, g1, b1, wup, wdn, wpg, bpg, wple, g2, b2) = weights
    return pl.pallas_call(
        functools.partial(_post_kernel, alpha=alpha, emit_orders=emit_orders),
        grid=(n_tok // tm,),
        in_specs=[rows(D_MODEL), rows(A_Q_W)]
        + [grouped(g) for g in range(N_GROUPS)] * 2
        + [rows(GATE_W), pl.BlockSpec((None, tm, PLE_DIM), lambda i: (layer, i, 0)),
           resident((A_Q_W, D_MODEL)), resident((B_GROUP_W, D_MODEL)),
           resident((D_MODEL, D_MODEL)), vec, vec,
           resident((D_MODEL, D_FF)), resident((D_FF, D_MODEL)),
           resident((D_MODEL, D_MODEL)), vec,
           resident((PLE_DIM, D_MODEL)), vec, vec],
        out_specs=out_specs,
        out_shape=out_shape,
        scratch_shapes=scratch_shapes,
        compiler_params=pltpu.CompilerParams(
            dimension_semantics=("arbitrary",), vmem_limit_bytes=VMEM_LIMIT_BYTES),
        name="post",
    )(x2d, oa, *[as_grouped(o, g) for g, o in enumerate(o_parts)],
      *[as_grouped(l, g) for g, l in enumerate(lse_parts)], gates, p_all,
      wa, wb, wo, g1, b1, wup, wdn, wpg, bpg, wple, g2, b2)


def kernel(x, p, w_in, b_gate, a_sink, w_branch_a, w_branch_b, w_out, ln1_g, ln1_b, w_up, w_down,
           w_ple_gate, b_ple_gate, w_ple, ln2_g, ln2_b):
    bsz, seq, d_model = x.shape
    depth = w_in.shape[0]
    assert d_model == D_MODEL and w_in.shape[2] == NAT_W + T_W
    assert seq % (TOKEN_BLOCK * max(DILATIONS)) == 0 and seq % IN_PROJ_ROWS == 0
    assert POST_ROWS % (2 * SUBLANES * max(DILATIONS)) == 0
    n_tok = bsz * seq
    alpha = (2 * depth) ** 0.25
    bf16 = jnp.bfloat16
    tables = _rope_tables(seq)
    p_all = p.reshape(depth, n_tok, PLE_DIM)
    x2d = x.reshape(n_tok, d_model)
    x_orders = _reorder(x2d, bsz, seq)
    for i in range(depth):
        w_nat, w_t = _in_proj_weights(w_in[i])
        flat = [xo.reshape(n_tok, d_model) for xo in x_orders]
        outs = _in_proj(flat, w_nat, w_t, b_gate[i][None, :], tables, seq)
        ka, qat, vat = outs[:3]
        gates = outs[-1]
        (oa,) = _attn(qat, ka, vat, a_sink[i], seq=seq, blocks_per_seq=seq // TOKEN_BLOCK,
                      n_q_heads=A_Q_HEADS, n_kv_heads=A_KV_HEADS, win=A_WINDOW, emit_lse=False,
                      name="attn_a")
        o_parts, lse_parts = [], []
        for g, (window, r) in enumerate(B_GROUPS):
            kg, qtg, vtg = outs[3 + 3 * g:6 + 3 * g]
            o, lse = _attn(qtg, kg, vtg, None, seq=seq,
                           blocks_per_seq=seq // r // TOKEN_BLOCK,
                           n_q_heads=B_HEADS_PER_GROUP, n_kv_heads=B_HEADS_PER_GROUP,
                           win=(window // 2) // r, emit_lse=True, name=f"attn_b{g}")
            o_parts.append(o)
            lse_parts.append(lse)
        weights = (w_branch_a[i].astype(bf16), w_branch_b[i].astype(bf16), w_out[i].astype(bf16),
                   ln1_g[i][None, :], ln1_b[i][None, :], w_up[i].astype(bf16),
                   w_down[i].astype(bf16), w_ple_gate[i].astype(bf16), b_ple_gate[i][None, :],
                   w_ple[i].astype(bf16), ln2_g[i][None, :], ln2_b[i][None, :])
        res = _post(x2d, oa, o_parts, lse_parts, gates, p_all, i, weights, alpha, bsz, seq,
                    emit_orders=i + 1 < depth)
        x2d, x_orders = res[0], res[1:]
    return x2d.reshape(bsz, seq, d_model)
```

```python
import functools
import math

import jax
import jax.numpy as jnp
import numpy as np
from jax import lax
from jax.experimental import pallas as pl
from jax.experimental.pallas import tpu as pltpu

D_MODEL = 1024
HEAD_DIM = 64
ROT_DIM = HEAD_DIM // 4
ROT_HALF = ROT_DIM // 2
ROPE_THETA = 500000.0

A_Q_HEADS = 8
A_KV_HEADS = 2
A_WINDOW = 128
B_GROUPS = ((128, 1), (512, 4), (2048, 16))
N_GROUPS = len(B_GROUPS)
B_HEADS_PER_GROUP = 4
DILATIONS = tuple(r for _, r in B_GROUPS)

A_Q_W = A_Q_HEADS * HEAD_DIM
A_KV_W = A_KV_HEADS * HEAD_DIM
B_GROUP_W = B_HEADS_PER_GROUP * HEAD_DIM
B_W = N_GROUPS * B_GROUP_W
GATE_W = 2 * D_MODEL
D_FF = 4 * D_MODEL
PLE_DIM = 256
LN_EPS = 1e-5

LANES = 128
SUBLANES = 8
TOKEN_BLOCK = 128
VMEM_LIMIT_BYTES = 56 * 1024 * 1024
NEG_BIG = -1e30
LOG2_E = math.log2(math.e)
LN_2 = math.log(2.0)

IN_PROJ_ROWS = 1024
POST_ROWS = 512
FF_CHUNK = 1024

_NAT_KA = 0
_NAT_KB = A_KV_W
_NAT_GATE = A_KV_W + B_W
NAT_W = _NAT_GATE + GATE_W
_T_QA = 0
_T_VA = A_Q_W
_T_B = A_Q_W + A_KV_W
T_W = _T_B + 2 * B_W


def _rope_tables(seq):
    pos = jnp.arange(seq, dtype=jnp.float32)
    inv_freq = ROPE_THETA ** (-jnp.arange(0, ROT_DIM, 2, dtype=jnp.float32) / ROT_DIM)
    ang = pos[:, None] * inv_freq[None, :]
    cos, sin = jnp.cos(ang), jnp.sin(ang)
    zeros = jnp.zeros((seq, HEAD_DIM - ROT_DIM), jnp.float32)
    zeros_h = jnp.zeros((seq, ROT_HALF), jnp.float32)
    reps = LANES // HEAD_DIM
    c = jnp.tile(jnp.concatenate([cos, cos, jnp.ones_like(zeros)], axis=-1), (1, reps))
    sa = jnp.tile(jnp.concatenate([-sin, zeros_h, zeros], axis=-1), (1, reps))
    sb = jnp.tile(jnp.concatenate([zeros_h, sin, zeros], axis=-1), (1, reps))

    def orders(t):
        return jnp.stack([t.reshape(seq // r, r, -1).transpose(1, 0, 2).reshape(seq, -1)
                          for r in DILATIONS])

    c, sa, sb, cos_o, sin_o = (orders(t) for t in (c, sa, sb, cos, sin))
    return c, sa, sb, cos_o.transpose(0, 2, 1), sin_o.transpose(0, 2, 1)


def _in_proj_weights(w):
    qa, ka, va, qb, kb, vb, gl = jnp.split(
        w, (A_Q_W, A_Q_W + A_KV_W, A_Q_W + 2 * A_KV_W, A_Q_W + 2 * A_KV_W + B_W,
            A_Q_W + 2 * A_KV_W + 2 * B_W, A_Q_W + 2 * A_KV_W + 3 * B_W), axis=1)
    scale = LOG2_E / math.sqrt(HEAD_DIM)
    w_nat = jnp.concatenate([ka, kb, gl], axis=1).astype(jnp.bfloat16)
    cols = [qa * scale, va]
    for g in range(N_GROUPS):
        cols += [qb[:, g * B_GROUP_W:(g + 1) * B_GROUP_W] * scale,
                 vb[:, g * B_GROUP_W:(g + 1) * B_GROUP_W]]
    w_t = jnp.concatenate(cols, axis=1).T.astype(jnp.bfloat16)
    return w_nat, w_t


def _fill_slabs(slab_ref, rows, val):
    for s in range(slab_ref.shape[0]):
        slab_ref[s, rows, :] = val[:, s * LANES:(s + 1) * LANES]


def _read_slabs(slab_ref, rows):
    return jnp.concatenate([slab_ref[s, rows, :] for s in range(slab_ref.shape[0])], axis=1)


def _slabs_to_class_major(dst_refs, slab_ref):
    n_slabs, rows, _ = slab_ref.shape
    for r, dst_ref in zip(DILATIONS[1:], dst_refs):
        n = rows // r
        for c in range(r):
            dst_ref[c] = jnp.concatenate(
                [slab_ref[s, pl.ds(c, n, stride=r), :] for s in range(n_slabs)],
                axis=1).astype(dst_ref.dtype)


def _scatter_token_order(src_ref, slab_ref, r):
    n = src_ref.shape[1]
    for c in range(r):
        blk = src_ref[c].astype(jnp.float32)
        for s in range(slab_ref.shape[0]):
            slab_ref[s, pl.ds(c, n, stride=r), :] = blk[:, s * LANES:(s + 1) * LANES]


def _reorder_kernel(x_ref, xb_ref, *rest):
    xp_refs, slab_ref = rest[:-1], rest[-1]
    x = x_ref[...]
    xb_ref[...] = x.astype(xb_ref.dtype)
    _fill_slabs(slab_ref, slice(None), x)
    _slabs_to_class_major(xp_refs, slab_ref)


def _x_order_specs(bsz, seq, tm):
    per_seq = seq // tm
    specs = [pl.BlockSpec((tm, D_MODEL), lambda i: (i, 0))]
    shapes = [jax.ShapeDtypeStruct((bsz * seq, D_MODEL), jnp.bfloat16)]
    for r in DILATIONS[1:]:
        specs.append(pl.BlockSpec((None, r, tm // r, D_MODEL),
                                  lambda i: (i // per_seq, 0, i % per_seq, 0)))
        shapes.append(jax.ShapeDtypeStruct((bsz, r, seq // r, D_MODEL), jnp.bfloat16))
    return specs, shapes


def _slab_scratch(rows, width):
    return pltpu.VMEM((width // LANES, rows, LANES), jnp.float32)


def _reorder(x2d, bsz, seq):
    tm = POST_ROWS
    specs, shapes = _x_order_specs(bsz, seq, tm)
    return pl.pallas_call(
        _reorder_kernel,
        grid=(bsz * seq // tm,),
        in_specs=[pl.BlockSpec((tm, D_MODEL), lambda i: (i, 0))],
        out_specs=specs,
        out_shape=shapes,
        scratch_shapes=[_slab_scratch(tm, D_MODEL)],
        compiler_params=pltpu.CompilerParams(
            dimension_semantics=("arbitrary",), vmem_limit_bytes=VMEM_LIMIT_BYTES),
        name="reorder",
    )(x2d)


def _rope_lanes(t, c, sa, sb):
    up = pltpu.roll(t, LANES - ROT_HALF, axis=1)
    down = pltpu.roll(t, ROT_HALF, axis=1)
    return t * c + up * sa + down * sb


def _rope_sublanes(t, cos, sin, n_heads):
    parts = []
    for h in range(n_heads):
        a = t[h * HEAD_DIM:h * HEAD_DIM + ROT_HALF]
        b = t[h * HEAD_DIM + ROT_HALF:h * HEAD_DIM + ROT_DIM]
        parts += [a * cos - b * sin, b * cos + a * sin, t[h * HEAD_DIM + ROT_DIM:(h + 1) * HEAD_DIM]]
    return jnp.concatenate(parts, axis=0)


def _store_token_blocks(dst_ref, val):
    for j in range(val.shape[1] // LANES):
        dst_ref[j] = val[:, j * LANES:(j + 1) * LANES].astype(dst_ref.dtype)


def _in_proj_kernel(*refs):
    x_refs = refs[:N_GROUPS]
    wn_ref, wt_ref, bg_ref, cl_ref, sal_ref, sbl_ref, ct_ref, st_ref = refs[N_GROUPS:N_GROUPS + 8]
    outs = refs[N_GROUPS + 8:]
    ka_ref, qat_ref, vat_ref = outs[:3]
    b_refs = [outs[3 + 3 * g:6 + 3 * g] for g in range(N_GROUPS)]
    gate_ref = outs[3 + 3 * N_GROUPS]
    f32 = jnp.float32
    nt = (((1,), (1,)), ((), ()))

    def k_rope(acc, order):
        c, sa, sb = cl_ref[order], sal_ref[order], sbl_ref[order]
        return jnp.concatenate(
            [_rope_lanes(acc[:, t * LANES:(t + 1) * LANES], c, sa, sb)
             for t in range(acc.shape[1] // LANES)], axis=1)

    x0 = x_refs[0][...]
    acc = jnp.dot(x0, wn_ref[:, _NAT_KA:_NAT_KA + A_KV_W], preferred_element_type=f32)
    ka_ref[...] = k_rope(acc, 0).astype(ka_ref.dtype)
    acc = lax.dot_general(wt_ref[_T_QA:_T_QA + A_Q_W, :], x0, nt, preferred_element_type=f32)
    _store_token_blocks(qat_ref, _rope_sublanes(acc, ct_ref[0], st_ref[0], A_Q_HEADS))
    acc = lax.dot_general(wt_ref[_T_VA:_T_VA + A_KV_W, :], x0, nt, preferred_element_type=f32)
    _store_token_blocks(vat_ref, acc)
    for g in range(N_GROUPS):
        xg = x_refs[g][...]
        k_ref, qt_ref, vt_ref = b_refs[g]
        col = _NAT_KB + g * B_GROUP_W
        acc = jnp.dot(xg, wn_ref[:, col:col + B_GROUP_W], preferred_element_type=f32)
        k_ref[...] = k_rope(acc, g).astype(k_ref.dtype)
        row = _T_B + 2 * g * B_GROUP_W
        acc = lax.dot_general(wt_ref[row:row + B_GROUP_W, :], xg, nt, preferred_element_type=f32)
        _store_token_blocks(qt_ref, _rope_sublanes(acc, ct_ref[g], st_ref[g], B_HEADS_PER_GROUP))
        acc = lax.dot_general(wt_ref[row + B_GROUP_W:row + 2 * B_GROUP_W, :], xg, nt,
                              preferred_element_type=f32)
        _store_token_blocks(vt_ref, acc)
    chunk = 512
    for j in range(GATE_W // chunk):
        z = jnp.dot(x0, wn_ref[:, _NAT_GATE + j * chunk:_NAT_GATE + (j + 1) * chunk],
                    preferred_element_type=f32)
        z = z + bg_ref[:, j * chunk:(j + 1) * chunk]
        gate_ref[:, j * chunk:(j + 1) * chunk] = (1.0 / (1.0 + jnp.exp(-z))).astype(gate_ref.dtype)


def _in_proj(x_orders, w_nat, w_t, b_gate, tables, seq):
    tm = IN_PROJ_ROWS
    n_tok = x_orders[0].shape[0]
    per_seq = seq // tm
    blocks = tm // TOKEN_BLOCK
    const = lambda i: (0, 0)
    rows = lambda w: pl.BlockSpec((tm, w), lambda i: (i, 0))
    tblocks = lambda r: pl.BlockSpec((blocks, r, TOKEN_BLOCK), lambda i: (i, 0, 0))
    lane_tab = pl.BlockSpec((N_GROUPS, tm, LANES), lambda i: (0, i % per_seq, 0))
    sub_tab = pl.BlockSpec((N_GROUPS, ROT_HALF, tm), lambda i: (0, 0, i % per_seq))
    bf16 = jnp.bfloat16
    nat = lambda w: jax.ShapeDtypeStruct((n_tok, w), bf16)
    tsp = lambda r: jax.ShapeDtypeStruct((n_tok // TOKEN_BLOCK, r, TOKEN_BLOCK), bf16)
    out_specs = [rows(A_KV_W), tblocks(A_Q_W), tblocks(A_KV_W)]
    out_shape = [nat(A_KV_W), tsp(A_Q_W), tsp(A_KV_W)]
    for _ in range(N_GROUPS):
        out_specs += [rows(B_GROUP_W), tblocks(B_GROUP_W), tblocks(B_GROUP_W)]
        out_shape += [nat(B_GROUP_W), tsp(B_GROUP_W), tsp(B_GROUP_W)]
    out_specs.append(rows(GATE_W))
    out_shape.append(nat(GATE_W))
    return pl.pallas_call(
        _in_proj_kernel,
        grid=(n_tok // tm,),
        in_specs=[rows(D_MODEL)] * N_GROUPS + [
            pl.BlockSpec((D_MODEL, NAT_W), const, pipeline_mode=pl.Buffered(1)),
            pl.BlockSpec((T_W, D_MODEL), const, pipeline_mode=pl.Buffered(1)),
            pl.BlockSpec((1, GATE_W), const),
            lane_tab, lane_tab, lane_tab, sub_tab, sub_tab,
        ],
        out_specs=out_specs,
        out_shape=out_shape,
        compiler_params=pltpu.CompilerParams(
            dimension_semantics=("arbitrary",), vmem_limit_bytes=VMEM_LIMIT_BYTES),
        name="in_proj",
    )(*x_orders, w_nat, w_t, b_gate, *tables)


_BIAS_MASKED = 3
ROUND_BLOCKS = 2


def _attn_bias(win):
    row = np.arange(TOKEN_BLOCK)[:, None]
    col = np.arange(2 * TOKEN_BLOCK)[None, :] % TOKEN_BLOCK
    tiles = [np.where(np.abs(col - row - t * TOKEN_BLOCK) <= win, 0.0, NEG_BIG)
             for t in (-1, 0, 1)]
    tiles.append(np.full((TOKEN_BLOCK, 2 * TOKEN_BLOCK), NEG_BIG))
    return jnp.asarray(np.stack(tiles), jnp.float32)


def _round_robin(streams):
    streams = list(streams)
    while streams:
        for s in list(streams):
            if next(s, StopIteration) is StopIteration:
                streams.remove(s)


def _fold_rows(op, tile, acc):
    for r in range(tile.shape[0] // SUBLANES):
        part = tile[r * SUBLANES:(r + 1) * SUBLANES, :]
        acc = part if acc is None else op(acc, part)
    return acc


def _attn_kernel(*refs, blocks_per_seq, n_q_heads, n_kv_heads, win, has_sink, emit_lse):
    refs = list(refs)
    qt_ref, k_ref, vt_ref, bias_ref = refs[:4]
    pos = 4
    sink_ref = None
    if has_sink:
        sink_ref = refs[pos]
        pos += 1
    o_ref = refs[pos]
    pos += 1
    lse_ref = None
    if emit_lse:
        lse_ref = refs[pos]
        pos += 1
    n_slots = 2 * ROUND_BLOCKS
    s_bufs, m_bufs, p_bufs, d_bufs, l_bufs = (
        refs[pos + n_slots * n:pos + n_slots * (n + 1)] for n in range(5))

    n_blocks = qt_ref.shape[0]
    n_rounds = n_blocks // ROUND_BLOCKS
    assert n_blocks % ROUND_BLOCKS == 0 and n_rounds % 2 == 0 and n_rounds >= 4
    grp = n_q_heads // n_kv_heads
    n_pairs = n_q_heads // 2
    tb = TOKEN_BLOCK
    offsets = (0,) if blocks_per_seq == 1 else (-1, 0, 1)
    f32, bf16 = jnp.float32, jnp.bfloat16
    lane = lax.broadcasted_iota(jnp.int32, (1, 2 * tb), 1)
    zeros = jnp.zeros((HEAD_DIM, tb), bf16)

    def heads_of(j):
        ha, hb = 2 * j, 2 * j + 1
        ka, kb = ha // grp, hb // grp
        assert ka // 2 == kb // 2
        return ha, hb, ka, kb

    def sink_of(j):
        ha, hb, _, _ = heads_of(j)
        return jnp.where(lane < tb, sink_ref[ha], sink_ref[hb]) * LOG2_E

    def tiles_of(i):
        local = i % blocks_per_seq
        out = {}
        for t in offsets:
            if t == 0:
                out[t] = (i, 1)
            else:
                ok = jnp.logical_and(local + t >= 0, local + t < blocks_per_seq)
                out[t] = (jnp.where(ok, i + t, i), jnp.where(ok, t + 1, _BIAS_MASKED))
        return out

    def scores(i, slot, pairs):
        tiles = tiles_of(i)
        for j in pairs:
            ha, hb, ka, kb = heads_of(j)
            qa = qt_ref[i, ha * HEAD_DIM:(ha + 1) * HEAD_DIM, :]
            qb = qt_ref[i, hb * HEAD_DIM:(hb + 1) * HEAD_DIM, :]
            halves = [jnp.concatenate([qa if ka % 2 == u else zeros,
                                       qb if kb % 2 == u else zeros], axis=1) for u in (0, 1)]
            rhs = jnp.concatenate(halves, axis=0)
            top = None
            for n, t in enumerate(offsets):
                blk, bias_idx = tiles[t]
                kt = k_ref[pl.ds(pl.multiple_of(blk * tb, tb), tb),
                           (ka // 2) * LANES:(ka // 2 + 1) * LANES]
                st = jnp.dot(kt, rhs, preferred_element_type=f32)
                if not (t == 0 and win >= tb - 1):
                    st = st + bias_ref[bias_idx]
                s_bufs[slot][j, n * tb:(n + 1) * tb, :] = st
                top = _fold_rows(jnp.maximum, st, top)
                yield
            m = jnp.max(top, axis=0, keepdims=True)
            if has_sink:
                m = jnp.maximum(m, sink_of(j))
            m_bufs[slot][j] = jnp.broadcast_to(m, (SUBLANES, 2 * tb))

    def softmax(slot, pairs):
        for j in pairs:
            m = m_bufs[slot][j, 0:1, :]
            tot = None
            for n in range(len(offsets)):
                e = jnp.exp2(s_bufs[slot][j, n * tb:(n + 1) * tb, :] - m)
                p_bufs[slot][j, n * tb:(n + 1) * tb, :] = e.astype(bf16)
                tot = _fold_rows(jnp.add, e, tot)
                yield
            denom = jnp.sum(tot, axis=0, keepdims=True)
            if has_sink:
                denom = denom + jnp.exp2(sink_of(j) - m)
            d_bufs[slot][j] = jnp.broadcast_to(denom, (SUBLANES, 2 * tb))
            if emit_lse:
                lse = (m + jnp.log2(denom)) * LN_2
                l_bufs[slot][j] = jnp.broadcast_to(lse, (SUBLANES, 2 * tb))

    def values(i, slot, pairs):
        tiles = tiles_of(i)
        q0 = pl.multiple_of(i * tb, tb)
        for j in pairs:
            _, _, ka, kb = heads_of(j)
            acc = None
            for n, t in enumerate(offsets):
                blk, _ = tiles[t]
                p = p_bufs[slot][j, n * tb:(n + 1) * tb, :]
                if ka == kb:
                    part = jnp.dot(vt_ref[blk, ka * HEAD_DIM:(ka + 1) * HEAD_DIM, :], p,
                                   preferred_element_type=f32)
                else:
                    part = jnp.concatenate(
                        [jnp.dot(vt_ref[blk, ka * HEAD_DIM:(ka + 1) * HEAD_DIM, :], p[:, :tb],
                                 preferred_element_type=f32),
                         jnp.dot(vt_ref[blk, kb * HEAD_DIM:(kb + 1) * HEAD_DIM, :], p[:, tb:],
                                 preferred_element_type=f32)], axis=1)
                acc = part if acc is None else acc + part
                yield
            ot = acc * (1.0 / d_bufs[slot][j, 0:1, :])
            pair = jnp.concatenate([ot[:, :tb], ot[:, tb:]], axis=0)
            o_ref[pl.ds(q0, tb), j * LANES:(j + 1) * LANES] = pair.T.astype(o_ref.dtype)
            if emit_lse:
                lse = l_bufs[slot][j, 0:1, :]
                both = jnp.concatenate([jnp.broadcast_to(lse[:, :tb], (HEAD_DIM, tb)),
                                        jnp.broadcast_to(lse[:, tb:], (HEAD_DIM, tb))], axis=0)
                lse_ref[pl.ds(q0, tb), j * LANES:(j + 1) * LANES] = both.T

    def do_round(r, parity, first=False, last=False):
        cur = [parity * ROUND_BLOCKS + b for b in range(ROUND_BLOCKS)]
        oth = [(1 - parity) * ROUND_BLOCKS + b for b in range(ROUND_BLOCKS)]
        for j in range(n_pairs):
            for b in range(ROUND_BLOCKS):
                streams = [softmax(cur[b], [j])]
                if not first:
                    streams.append(values((r - 1) * ROUND_BLOCKS + b, oth[b], [j]))
                if not last:
                    streams.append(scores((r + 1) * ROUND_BLOCKS + b, oth[b], [j]))
                _round_robin(streams)

    all_pairs = range(n_pairs)
    for b in range(ROUND_BLOCKS):
        _round_robin([scores(b, b, all_pairs)])
    do_round(0, 0, first=True)

    def body(k, carry):
        do_round(2 * k + 1, 1)
        do_round(2 * k + 2, 0)
        return carry

    lax.fori_loop(0, (n_rounds - 2) // 2, body, 0)
    do_round(n_rounds - 1, 1, last=True)
    for b in range(ROUND_BLOCKS):
        _round_robin([values((n_rounds - 1) * ROUND_BLOCKS + b, ROUND_BLOCKS + b, all_pairs)])


def _attn(qt, k, vt, sink, *, seq, blocks_per_seq, n_q_heads, n_kv_heads, win, emit_lse, name):
    n_tok = k.shape[0]
    blocks = seq // TOKEN_BLOCK
    qw, kw = n_q_heads * HEAD_DIM, n_kv_heads * HEAD_DIM
    n_pairs = n_q_heads // 2
    n_tiles = 1 if blocks_per_seq == 1 else 3
    kern = functools.partial(_attn_kernel, blocks_per_seq=blocks_per_seq, n_q_heads=n_q_heads,
                             n_kv_heads=n_kv_heads, win=win, has_sink=sink is not None,
                             emit_lse=emit_lse)
    in_specs = [pl.BlockSpec((blocks, qw, TOKEN_BLOCK), lambda b: (b, 0, 0)),
                pl.BlockSpec((seq, kw), lambda b: (b, 0)),
                pl.BlockSpec((blocks, kw, TOKEN_BLOCK), lambda b: (b, 0, 0)),
                pl.BlockSpec((_BIAS_MASKED + 1, TOKEN_BLOCK, 2 * TOKEN_BLOCK), lambda b: (0, 0, 0))]
    args = [qt, k, vt, _attn_bias(win)]
    if sink is not None:
        in_specs.append(pl.BlockSpec(memory_space=pltpu.SMEM))
        args.append(sink)
    out_specs = [pl.BlockSpec((seq, qw), lambda b: (b, 0))]
    out_shape = [jax.ShapeDtypeStruct((n_tok, qw), jnp.bfloat16)]
    if emit_lse:
        out_specs.append(pl.BlockSpec((seq, qw), lambda b: (b, 0)))
        out_shape.append(jax.ShapeDtypeStruct((n_tok, qw), jnp.float32))
    big = (n_pairs, n_tiles * TOKEN_BLOCK, 2 * TOKEN_BLOCK)
    small = (n_pairs, SUBLANES, 2 * TOKEN_BLOCK)
    n_slots = 2 * ROUND_BLOCKS
    scratch_shapes = ([pltpu.VMEM(big, jnp.float32)] * n_slots
                      + [pltpu.VMEM(small, jnp.float32)] * n_slots
                      + [pltpu.VMEM(big, jnp.bfloat16)] * n_slots
                      + [pltpu.VMEM(small, jnp.float32)] * (2 * n_slots))
    return pl.pallas_call(
        kern,
        grid=(n_tok // seq,),
        in_specs=in_specs,
        out_specs=out_specs,
        out_shape=out_shape,
        scratch_shapes=scratch_shapes,
        compiler_params=pltpu.CompilerParams(
            dimension_semantics=("arbitrary",), vmem_limit_bytes=VMEM_LIMIT_BYTES),
        name=name,
    )(*args)


def _layer_norm(y, g, b):
    mu = jnp.mean(y, axis=-1, keepdims=True)
    d = y - mu
    var = jnp.mean(d * d, axis=-1, keepdims=True)
    return d * lax.rsqrt(var + LN_EPS) * g + b


def _run_halves(halves, lead):
    first, second = halves
    for _ in range(lead):
        next(first)
    _round_robin([first, second])


def _mix_kernel(*refs, alpha):
    (x_ref, oa_ref) = refs[:2]
    o_refs = refs[2:2 + N_GROUPS]
    l_refs = refs[2 + N_GROUPS:2 + 2 * N_GROUPS]
    gate_ref = refs[2 + 2 * N_GROUPS]
    wa_ref, wb_ref, wo_ref, g1_ref, b1_ref = refs[3 + 2 * N_GROUPS:8 + 2 * N_GROUPS]
    x1_ref, x1b_ref = refs[8 + 2 * N_GROUPS:10 + 2 * N_GROUPS]
    slabs = refs[10 + 2 * N_GROUPS:]
    f32, bf16 = jnp.float32, jnp.bfloat16

    for g in range(1, N_GROUPS):
        _scatter_token_order(o_refs[g], slabs[2 * g - 2], DILATIONS[g])
        _scatter_token_order(l_refs[g], slabs[2 * g - 1], DILATIONS[g])

    def half(rows):
        os_, ls_ = [o_refs[0][rows, :].astype(f32)], [l_refs[0][rows, :]]
        for g in range(1, N_GROUPS):
            os_.append(_read_slabs(slabs[2 * g - 2], rows))
            ls_.append(_read_slabs(slabs[2 * g - 1], rows))
        mx = functools.reduce(jnp.maximum, ls_)
        es = [jnp.exp(l - mx) for l in ls_]
        ob = (functools.reduce(jnp.add, [e * o for e, o in zip(es, os_)])
              / functools.reduce(jnp.add, es))
        yield
        ya = jnp.dot(oa_ref[rows, :], wa_ref[...], preferred_element_type=f32)
        yb = jnp.dot(ob.astype(bf16), wb_ref[...], preferred_element_type=f32)
        ga = gate_ref[rows, :D_MODEL].astype(f32)
        gb = gate_ref[rows, D_MODEL:].astype(f32)
        merged = (ga * ya + gb * yb).astype(bf16)
        h = jnp.dot(merged, wo_ref[...], preferred_element_type=f32)
        yield
        x1 = _layer_norm(alpha * x_ref[rows, :] + h, g1_ref[...], b1_ref[...])
        x1_ref[rows, :] = x1
        x1b_ref[rows, :] = x1.astype(bf16)
        yield

    tm = x_ref.shape[0]
    _run_halves([half(slice(0, tm // 2)), half(slice(tm // 2, tm))], lead=1)


def _mlp_kernel(*refs, alpha, emit_orders):
    (x1_ref, x1b_ref, p_ref, wup_ref, wdn_ref, wpg_ref, bpg_ref, wple_ref, g2_ref,
     b2_ref) = refs[:10]
    out_ref = refs[10]
    order_refs = refs[11:11 + N_GROUPS] if emit_orders else ()
    slab_ref = refs[-1] if emit_orders else None
    f32, bf16 = jnp.float32, jnp.bfloat16

    def half(rows):
        x1b = x1b_ref[rows, :]
        acc = alpha * x1_ref[rows, :]
        for c in range(D_FF // FF_CHUNK):
            cols = slice(c * FF_CHUNK, (c + 1) * FF_CHUNK)
            up = jnp.dot(x1b, wup_ref[:, cols], preferred_element_type=f32)
            act = jnp.square(jnp.maximum(up, 0.0)).astype(bf16)
            acc = acc + jnp.dot(act, wdn_ref[cols, :], preferred_element_type=f32)
            yield
        zg = jnp.dot(x1b, wpg_ref[...], preferred_element_type=f32) + bpg_ref[...]
        gate = 1.0 / (1.0 + jnp.exp(-zg))
        ple = gate * jnp.dot(p_ref[rows, :].astype(bf16), wple_ref[...],
                             preferred_element_type=f32)
        x2 = _layer_norm(acc + ple, g2_ref[...], b2_ref[...])
        out_ref[rows, :] = x2
        if emit_orders:
            order_refs[0][rows, :] = x2.astype(bf16)
            _fill_slabs(slab_ref, rows, x2)
        yield

    tm = x1_ref.shape[0]
    _run_halves([half(slice(0, tm // 2)), half(slice(tm // 2, tm))], lead=2)
    if emit_orders:
        _slabs_to_class_major(order_refs[1:], slab_ref)


def _post(x2d, oa, o_parts, lse_parts, gates, p_all, layer, weights, alpha, bsz, seq, emit_orders):
    tm = POST_ROWS
    n_tok = x2d.shape[0]
    per_seq = seq // tm
    const = lambda i: (0, 0)
    rows = lambda w: pl.BlockSpec((tm, w), lambda i: (i, 0))
    resident = lambda shape: pl.BlockSpec(shape, const, pipeline_mode=pl.Buffered(1))
    vec = pl.BlockSpec((1, D_MODEL), const)
    params = pltpu.CompilerParams(
        dimension_semantics=("arbitrary",), vmem_limit_bytes=VMEM_LIMIT_BYTES)

    def grouped(g):
        r = DILATIONS[g]
        if r == 1:
            return rows(B_GROUP_W)
        return pl.BlockSpec((None, r, tm // r, B_GROUP_W),
                            lambda i: (i // per_seq, 0, i % per_seq, 0))

    def as_grouped(a, g):
        r = DILATIONS[g]
        return a if r == 1 else a.reshape(bsz, r, seq // r, B_GROUP_W)

    (wa, wb, wo, g1, b1, wup, wdn, wpg, bpg, wple, g2, b2) = weights
    x1, x1b = pl.pallas_call(
        functools.partial(_mix_kernel, alpha=alpha),
        grid=(n_tok // tm,),
        in_specs=[rows(D_MODEL), rows(A_Q_W)]
        + [grouped(g) for g in range(N_GROUPS)] * 2
        + [rows(GATE_W),
           resident((A_Q_W, D_MODEL)), resident((B_GROUP_W, D_MODEL)),
           resident((D_MODEL, D_MODEL)), vec, vec],
        out_specs=[rows(D_MODEL), rows(D_MODEL)],
        out_shape=[jax.ShapeDtypeStruct((n_tok, D_MODEL), jnp.float32),
                   jax.ShapeDtypeStruct((n_tok, D_MODEL), jnp.bfloat16)],
        scratch_shapes=[_slab_scratch(tm, B_GROUP_W)] * (2 * (N_GROUPS - 1)),
        compiler_params=params,
        name="mix",
    )(x2d, oa, *[as_grouped(o, g) for g, o in enumerate(o_parts)],
      *[as_grouped(l, g) for g, l in enumerate(lse_parts)], gates, wa, wb, wo, g1, b1)

    out_specs = [rows(D_MODEL)]
    out_shape = [jax.ShapeDtypeStruct((n_tok, D_MODEL), jnp.float32)]
    scratch_shapes = []
    if emit_orders:
        specs, shapes = _x_order_specs(bsz, seq, tm)
        out_specs += specs
        out_shape += shapes
        scratch_shapes = [_slab_scratch(tm, D_MODEL)]
    return pl.pallas_call(
        functools.partial(_mlp_kernel, alpha=alpha, emit_orders=emit_orders),
        grid=(n_tok // tm,),
        in_specs=[rows(D_MODEL), rows(D_MODEL),
                  pl.BlockSpec((None, tm, PLE_DIM), lambda i: (layer, i, 0)),
                  resident((D_MODEL, D_FF)), resident((D_FF, D_MODEL)),
                  resident((D_MODEL, D_MODEL)), vec,
                  resident((PLE_DIM, D_MODEL)), vec, vec],
        out_specs=out_specs,
        out_shape=out_shape,
        scratch_shapes=scratch_shapes,
        compiler_params=params,
        name="mlp",
    )(x1, x1b, p_all, wup, wdn, wpg, bpg, wple, g2, b2)


def kernel(x, p, w_in, b_gate, a_sink, w_branch_a, w_branch_b, w_out, ln1_g, ln1_b, w_up, w_down,
           w_ple_gate, b_ple_gate, w_ple, ln2_g, ln2_b):
    bsz, seq, d_model = x.shape
    depth = w_in.shape[0]
    assert d_model == D_MODEL and w_in.shape[2] == NAT_W + T_W
    assert seq % (TOKEN_BLOCK * max(DILATIONS)) == 0 and seq % IN_PROJ_ROWS == 0
    assert POST_ROWS % (2 * SUBLANES * max(DILATIONS)) == 0
    n_tok = bsz * seq
    alpha = (2 * depth) ** 0.25
    bf16 = jnp.bfloat16
    tables = _rope_tables(seq)
    p_all = p.reshape(depth, n_tok, PLE_DIM)
    x2d = x.reshape(n_tok, d_model)
    x_orders = _reorder(x2d, bsz, seq)
    for i in range(depth):
        w_nat, w_t = _in_proj_weights(w_in[i])
        flat = [xo.reshape(n_tok, d_model) for xo in x_orders]
        outs = _in_proj(flat, w_nat, w_t, b_gate[i][None, :], tables, seq)
        ka, qat, vat = outs[:3]
        gates = outs[-1]
        (oa,) = _attn(qat, ka, vat, a_sink[i], seq=seq, blocks_per_seq=seq // TOKEN_BLOCK,
                      n_q_heads=A_Q_HEADS, n_kv_heads=A_KV_HEADS, win=A_WINDOW, emit_lse=False,
                      name="attn_a")
        o_parts, lse_parts = [], []
        for g, (window, r) in enumerate(B_GROUPS):
            kg, qtg, vtg = outs[3 + 3 * g:6 + 3 * g]
            o, lse = _attn(qtg, kg, vtg, None, seq=seq,
                           blocks_per_seq=seq // r // TOKEN_BLOCK,
                           n_q_heads=B_HEADS_PER_GROUP, n_kv_heads=B_HEADS_PER_GROUP,
                           win=(window // 2) // r, emit_lse=True, name=f"attn_b{g}")
            o_parts.append(o)
            lse_parts.append(lse)
        weights = (w_branch_a[i].astype(bf16), w_branch_b[i].astype(bf16), w_out[i].astype(bf16),
                   ln1_g[i][None, :], ln1_b[i][None, :], w_up[i].astype(bf16),
                   w_down[i].astype(bf16), w_ple_gate[i].astype(bf16), b_ple_gate[i][None, :],
                   w_ple[i].astype(bf16), ln2_g[i][None, :], ln2_b[i][None, :])
        res = _post(x2d, oa, o_parts, lse_parts, gates, p_all, i, weights, alpha, bsz, seq,
                    emit_orders=i + 1 < depth)
        x2d, x_orders = res[0], res[1:]
    return x2d.reshape(bsz, seq, d_model)
```

```python
import functools
import math

import jax
import jax.numpy as jnp
import numpy as np
from jax import lax
from jax.experimental import pallas as pl
from jax.experimental.pallas import tpu as pltpu

D_MODEL = 1024
HEAD_DIM = 64
ROT_DIM = HEAD_DIM // 4
ROT_HALF = ROT_DIM // 2
ROPE_THETA = 500000.0

A_Q_HEADS = 8
A_KV_HEADS = 2
A_WINDOW = 128
B_GROUPS = ((128, 1), (512, 4), (2048, 16))
N_GROUPS = len(B_GROUPS)
B_HEADS_PER_GROUP = 4
DILATIONS = tuple(r for _, r in B_GROUPS)

A_Q_W = A_Q_HEADS * HEAD_DIM
A_KV_W = A_KV_HEADS * HEAD_DIM
B_GROUP_W = B_HEADS_PER_GROUP * HEAD_DIM
B_W = N_GROUPS * B_GROUP_W
GATE_W = 2 * D_MODEL
D_FF = 4 * D_MODEL
PLE_DIM = 256
LN_EPS = 1e-5

LANES = 128
SUBLANES = 8
TOKEN_BLOCK = 128
VMEM_LIMIT_BYTES = 56 * 1024 * 1024
NEG_BIG = -1e30
LOG2_E = math.log2(math.e)
LN_2 = math.log(2.0)

IN_PROJ_ROWS = 1024
POST_ROWS = 512
FF_CHUNK = 1024

_NAT_KA = 0
_NAT_KB = A_KV_W
_NAT_GATE = A_KV_W + B_W
NAT_W = _NAT_GATE + GATE_W
_T_QA = 0
_T_VA = A_Q_W
_T_B = A_Q_W + A_KV_W
T_W = _T_B + 2 * B_W


def _rope_tables(seq):
    pos = jnp.arange(seq, dtype=jnp.float32)
    inv_freq = ROPE_THETA ** (-jnp.arange(0, ROT_DIM, 2, dtype=jnp.float32) / ROT_DIM)
    ang = pos[:, None] * inv_freq[None, :]
    cos, sin = jnp.cos(ang), jnp.sin(ang)
    zeros = jnp.zeros((seq, HEAD_DIM - ROT_DIM), jnp.float32)
    zeros_h = jnp.zeros((seq, ROT_HALF), jnp.float32)
    reps = LANES // HEAD_DIM
    c = jnp.tile(jnp.concatenate([cos, cos, jnp.ones_like(zeros)], axis=-1), (1, reps))
    sa = jnp.tile(jnp.concatenate([-sin, zeros_h, zeros], axis=-1), (1, reps))
    sb = jnp.tile(jnp.concatenate([zeros_h, sin, zeros], axis=-1), (1, reps))

    def orders(t):
        return jnp.stack([t.reshape(seq // r, r, -1).transpose(1, 0, 2).reshape(seq, -1)
                          for r in DILATIONS])

    c, sa, sb, cos_o, sin_o = (orders(t) for t in (c, sa, sb, cos, sin))
    return c, sa, sb, cos_o.transpose(0, 2, 1), sin_o.transpose(0, 2, 1)


def _in_proj_weights(w):
    qa, ka, va, qb, kb, vb, gl = jnp.split(
        w, (A_Q_W, A_Q_W + A_KV_W, A_Q_W + 2 * A_KV_W, A_Q_W + 2 * A_KV_W + B_W,
            A_Q_W + 2 * A_KV_W + 2 * B_W, A_Q_W + 2 * A_KV_W + 3 * B_W), axis=1)
    scale = LOG2_E / math.sqrt(HEAD_DIM)
    w_nat = jnp.concatenate([ka, kb, gl], axis=1).astype(jnp.bfloat16)
    cols = [qa * scale, va]
    for g in range(N_GROUPS):
        cols += [qb[:, g * B_GROUP_W:(g + 1) * B_GROUP_W] * scale,
                 vb[:, g * B_GROUP_W:(g + 1) * B_GROUP_W]]
    w_t = jnp.concatenate(cols, axis=1).T.astype(jnp.bfloat16)
    return w_nat, w_t


def _fill_slabs(slab_ref, rows, val):
    for s in range(slab_ref.shape[0]):
        slab_ref[s, rows, :] = val[:, s * LANES:(s + 1) * LANES]


def _read_slabs(slab_ref, rows):
    return jnp.concatenate([slab_ref[s, rows, :] for s in range(slab_ref.shape[0])], axis=1)


def _slabs_to_class_major(dst_refs, slab_ref):
    n_slabs, rows, _ = slab_ref.shape
    for r, dst_ref in zip(DILATIONS[1:], dst_refs):
        n = rows // r
        for c in range(r):
            dst_ref[c] = jnp.concatenate(
                [slab_ref[s, pl.ds(c, n, stride=r), :] for s in range(n_slabs)],
                axis=1).astype(dst_ref.dtype)


def _scatter_token_order(src_ref, slab_ref, r):
    n = src_ref.shape[1]
    for c in range(r):
        blk = src_ref[c].astype(jnp.float32)
        for s in range(slab_ref.shape[0]):
            slab_ref[s, pl.ds(c, n, stride=r), :] = blk[:, s * LANES:(s + 1) * LANES]


def _reorder_kernel(x_ref, xb_ref, *rest):
    xp_refs, slab_ref = rest[:-1], rest[-1]
    x = x_ref[...]
    xb_ref[...] = x.astype(xb_ref.dtype)
    _fill_slabs(slab_ref, slice(None), x)
    _slabs_to_class_major(xp_refs, slab_ref)


def _x_order_specs(bsz, seq, tm):
    per_seq = seq // tm
    specs = [pl.BlockSpec((tm, D_MODEL), lambda i: (i, 0))]
    shapes = [jax.ShapeDtypeStruct((bsz * seq, D_MODEL), jnp.bfloat16)]
    for r in DILATIONS[1:]:
        specs.append(pl.BlockSpec((None, r, tm // r, D_MODEL),
                                  lambda i: (i // per_seq, 0, i % per_seq, 0)))
        shapes.append(jax.ShapeDtypeStruct((bsz, r, seq // r, D_MODEL), jnp.bfloat16))
    return specs, shapes


def _slab_scratch(rows, width):
    return pltpu.VMEM((width // LANES, rows, LANES), jnp.float32)


def _reorder(x2d, bsz, seq):
    tm = POST_ROWS
    specs, shapes = _x_order_specs(bsz, seq, tm)
    return pl.pallas_call(
        _reorder_kernel,
        grid=(bsz * seq // tm,),
        in_specs=[pl.BlockSpec((tm, D_MODEL), lambda i: (i, 0))],
        out_specs=specs,
        out_shape=shapes,
        scratch_shapes=[_slab_scratch(tm, D_MODEL)],
        compiler_params=pltpu.CompilerParams(
            dimension_semantics=("arbitrary",), vmem_limit_bytes=VMEM_LIMIT_BYTES),
        name="reorder",
    )(x2d)


def _rope_lanes(t, c, sa, sb):
    up = pltpu.roll(t, LANES - ROT_HALF, axis=1)
    down = pltpu.roll(t, ROT_HALF, axis=1)
    return t * c + up * sa + down * sb


def _rope_sublanes(t, cos, sin, n_heads):
    parts = []
    for h in range(n_heads):
        a = t[h * HEAD_DIM:h * HEAD_DIM + ROT_HALF]
        b = t[h * HEAD_DIM + ROT_HALF:h * HEAD_DIM + ROT_DIM]
        parts += [a * cos - b * sin, b * cos + a * sin, t[h * HEAD_DIM + ROT_DIM:(h + 1) * HEAD_DIM]]
    return jnp.concatenate(parts, axis=0)


def _store_token_blocks(dst_ref, val):
    for j in range(val.shape[1] // LANES):
        dst_ref[j] = val[:, j * LANES:(j + 1) * LANES].astype(dst_ref.dtype)


def _in_proj_kernel(*refs):
    x_refs = refs[:N_GROUPS]
    wn_ref, wt_ref, bg_ref, cl_ref, sal_ref, sbl_ref, ct_ref, st_ref = refs[N_GROUPS:N_GROUPS + 8]
    outs = refs[N_GROUPS + 8:]
    ka_ref, qat_ref, vat_ref = outs[:3]
    b_refs = [outs[3 + 3 * g:6 + 3 * g] for g in range(N_GROUPS)]
    gate_ref = outs[3 + 3 * N_GROUPS]
    f32 = jnp.float32
    nt = (((1,), (1,)), ((), ()))

    def k_rope(acc, order):
        c, sa, sb = cl_ref[order], sal_ref[order], sbl_ref[order]
        return jnp.concatenate(
            [_rope_lanes(acc[:, t * LANES:(t + 1) * LANES], c, sa, sb)
             for t in range(acc.shape[1] // LANES)], axis=1)

    x0 = x_refs[0][...]
    acc = jnp.dot(x0, wn_ref[:, _NAT_KA:_NAT_KA + A_KV_W], preferred_element_type=f32)
    ka_ref[...] = k_rope(acc, 0).astype(ka_ref.dtype)
    acc = lax.dot_general(wt_ref[_T_QA:_T_QA + A_Q_W, :], x0, nt, preferred_element_type=f32)
    _store_token_blocks(qat_ref, _rope_sublanes(acc, ct_ref[0], st_ref[0], A_Q_HEADS))
    acc = lax.dot_general(wt_ref[_T_VA:_T_VA + A_KV_W, :], x0, nt, preferred_element_type=f32)
    _store_token_blocks(vat_ref, acc)
    for g in range(N_GROUPS):
        xg = x_refs[g][...]
        k_ref, qt_ref, vt_ref = b_refs[g]
        col = _NAT_KB + g * B_GROUP_W
        acc = jnp.dot(xg, wn_ref[:, col:col + B_GROUP_W], preferred_element_type=f32)
        k_ref[...] = k_rope(acc, g).astype(k_ref.dtype)
        row = _T_B + 2 * g * B_GROUP_W
        acc = lax.dot_general(wt_ref[row:row + B_GROUP_W, :], xg, nt, preferred_element_type=f32)
        _store_token_blocks(qt_ref, _rope_sublanes(acc, ct_ref[g], st_ref[g], B_HEADS_PER_GROUP))
        acc = lax.dot_general(wt_ref[row + B_GROUP_W:row + 2 * B_GROUP_W, :], xg, nt,
                              preferred_element_type=f32)
        _store_token_blocks(vt_ref, acc)
    chunk = 512
    for j in range(GATE_W // chunk):
        z = jnp.dot(x0, wn_ref[:, _NAT_GATE + j * chunk:_NAT_GATE + (j + 1) * chunk],
                    preferred_element_type=f32)
        z = z + bg_ref[:, j * chunk:(j + 1) * chunk]
        gate_ref[:, j * chunk:(j + 1) * chunk] = (1.0 / (1.0 + jnp.exp(-z))).astype(gate_ref.dtype)


def _in_proj(x_orders, w_nat, w_t, b_gate, tables, seq):
    tm = IN_PROJ_ROWS
    n_tok = x_orders[0].shape[0]
    per_seq = seq // tm
    blocks = tm // TOKEN_BLOCK
    const = lambda i: (0, 0)
    rows = lambda w: pl.BlockSpec((tm, w), lambda i: (i, 0))
    tblocks = lambda r: pl.BlockSpec((blocks, r, TOKEN_BLOCK), lambda i: (i, 0, 0))
    lane_tab = pl.BlockSpec((N_GROUPS, tm, LANES), lambda i: (0, i % per_seq, 0))
    sub_tab = pl.BlockSpec((N_GROUPS, ROT_HALF, tm), lambda i: (0, 0, i % per_seq))
    bf16 = jnp.bfloat16
    nat = lambda w: jax.ShapeDtypeStruct((n_tok, w), bf16)
    tsp = lambda r: jax.ShapeDtypeStruct((n_tok // TOKEN_BLOCK, r, TOKEN_BLOCK), bf16)
    out_specs = [rows(A_KV_W), tblocks(A_Q_W), tblocks(A_KV_W)]
    out_shape = [nat(A_KV_W), tsp(A_Q_W), tsp(A_KV_W)]
    for _ in range(N_GROUPS):
        out_specs += [rows(B_GROUP_W), tblocks(B_GROUP_W), tblocks(B_GROUP_W)]
        out_shape += [nat(B_GROUP_W), tsp(B_GROUP_W), tsp(B_GROUP_W)]
    out_specs.append(rows(GATE_W))
    out_shape.append(nat(GATE_W))
    return pl.pallas_call(
        _in_proj_kernel,
        grid=(n_tok // tm,),
        in_specs=[rows(D_MODEL)] * N_GROUPS + [
            pl.BlockSpec((D_MODEL, NAT_W), const, pipeline_mode=pl.Buffered(1)),
            pl.BlockSpec((T_W, D_MODEL), const, pipeline_mode=pl.Buffered(1)),
            pl.BlockSpec((1, GATE_W), const),
            lane_tab, lane_tab, lane_tab, sub_tab, sub_tab,
        ],
        out_specs=out_specs,
        out_shape=out_shape,
        compiler_params=pltpu.CompilerParams(
            dimension_semantics=("arbitrary",), vmem_limit_bytes=VMEM_LIMIT_BYTES),
        name="in_proj",
    )(*x_orders, w_nat, w_t, b_gate, *tables)


_BIAS_MASKED = 3
ROUND_BLOCKS = 2


def _attn_bias(win):
    row = np.arange(TOKEN_BLOCK)[:, None]
    col = np.arange(2 * TOKEN_BLOCK)[None, :] % TOKEN_BLOCK
    tiles = [np.where(np.abs(col - row - t * TOKEN_BLOCK) <= win, 0.0, NEG_BIG)
             for t in (-1, 0, 1)]
    tiles.append(np.full((TOKEN_BLOCK, 2 * TOKEN_BLOCK), NEG_BIG))
    return jnp.asarray(np.stack(tiles), jnp.float32)


def _reach(win):
    pack = 2 * SUBLANES
    return min(TOKEN_BLOCK, -(-win // pack) * pack)


def _round_robin(streams):
    streams = list(streams)
    while streams:
        for s in list(streams):
            if next(s, StopIteration) is StopIteration:
                streams.remove(s)


def _fold_rows(op, tile, acc):
    for r in range(tile.shape[0] // SUBLANES):
        part = tile[r * SUBLANES:(r + 1) * SUBLANES, :]
        acc = part if acc is None else op(acc, part)
    return acc


def _attn_kernel(*refs, blocks_per_seq, n_q_heads, n_kv_heads, win, has_sink, emit_lse):
    refs = list(refs)
    qt_ref, k_ref, vt_ref, bias_ref = refs[:4]
    pos = 4
    sink_ref = None
    if has_sink:
        sink_ref = refs[pos]
        pos += 1
    o_ref = refs[pos]
    pos += 1
    lse_ref = None
    if emit_lse:
        lse_ref = refs[pos]
        pos += 1
    n_slots = 2 * ROUND_BLOCKS
    s_bufs, m_bufs, p_bufs, d_bufs, l_bufs = (
        refs[pos + n_slots * n:pos + n_slots * (n + 1)] for n in range(5))

    n_blocks = qt_ref.shape[0]
    n_rounds = n_blocks // ROUND_BLOCKS
    assert n_blocks % ROUND_BLOCKS == 0 and n_rounds % 2 == 0 and n_rounds >= 4
    grp = n_q_heads // n_kv_heads
    n_pairs = n_q_heads // 2
    tb = TOKEN_BLOCK
    offsets = (0,) if blocks_per_seq == 1 else (-1, 0, 1)
    reach = _reach(win)
    key_tiles, row0 = [], 0
    for t in offsets:
        n_rows = tb if t == 0 else reach
        key_tiles.append((t, tb - reach if t < 0 else 0, n_rows, row0))
        row0 += n_rows
    f32, bf16 = jnp.float32, jnp.bfloat16
    lane = lax.broadcasted_iota(jnp.int32, (1, 2 * tb), 1)
    zeros = jnp.zeros((HEAD_DIM, tb), bf16)

    def heads_of(j):
        ha, hb = 2 * j, 2 * j + 1
        ka, kb = ha // grp, hb // grp
        assert ka // 2 == kb // 2
        return ha, hb, ka, kb

    def sink_of(j):
        ha, hb, _, _ = heads_of(j)
        return jnp.where(lane < tb, sink_ref[ha], sink_ref[hb]) * LOG2_E

    def tiles_of(i):
        local = i % blocks_per_seq
        out = {}
        for t in offsets:
            if t == 0:
                out[t] = (i, 1)
            else:
                ok = jnp.logical_and(local + t >= 0, local + t < blocks_per_seq)
                out[t] = (jnp.where(ok, i + t, i), jnp.where(ok, t + 1, _BIAS_MASKED))
        return out

    def scores(i, slot, pairs):
        tiles = tiles_of(i)
        for j in pairs:
            ha, hb, ka, kb = heads_of(j)
            qa = qt_ref[i, ha * HEAD_DIM:(ha + 1) * HEAD_DIM, :]
            qb = qt_ref[i, hb * HEAD_DIM:(hb + 1) * HEAD_DIM, :]
            halves = [jnp.concatenate([qa if ka % 2 == u else zeros,
                                       qb if kb % 2 == u else zeros], axis=1) for u in (0, 1)]
            rhs = jnp.concatenate(halves, axis=0)
            top = None
            for t, k_off, n_rows, row0 in key_tiles:
                blk, bias_idx = tiles[t]
                kt = k_ref[pl.ds(pl.multiple_of(blk * tb + k_off, 2 * SUBLANES), n_rows),
                           (ka // 2) * LANES:(ka // 2 + 1) * LANES]
                st = jnp.dot(kt, rhs, preferred_element_type=f32)
                if not (t == 0 and win >= tb - 1):
                    st = st + bias_ref[bias_idx, k_off:k_off + n_rows, :]
                s_bufs[slot][j, row0:row0 + n_rows, :] = st
                top = _fold_rows(jnp.maximum, st, top)
                yield
            m = jnp.max(top, axis=0, keepdims=True)
            if has_sink:
                m = jnp.maximum(m, sink_of(j))
            m_bufs[slot][j] = jnp.broadcast_to(m, (SUBLANES, 2 * tb))

    def softmax(slot, pairs):
        for j in pairs:
            m = m_bufs[slot][j, 0:1, :]
            tot = None
            for _, _, n_rows, row0 in key_tiles:
                e = jnp.exp2(s_bufs[slot][j, row0:row0 + n_rows, :] - m)
                p_bufs[slot][j, row0:row0 + n_rows, :] = e.astype(bf16)
                tot = _fold_rows(jnp.add, e, tot)
                yield
            denom = jnp.sum(tot, axis=0, keepdims=True)
            if has_sink:
                denom = denom + jnp.exp2(sink_of(j) - m)
            d_bufs[slot][j] = jnp.broadcast_to(denom, (SUBLANES, 2 * tb))
            if emit_lse:
                lse = (m + jnp.log2(denom)) * LN_2
                l_bufs[slot][j] = jnp.broadcast_to(lse, (SUBLANES, 2 * tb))

    def values(i, slot, pairs):
        tiles = tiles_of(i)
        q0 = pl.multiple_of(i * tb, tb)
        for j in pairs:
            _, _, ka, kb = heads_of(j)
            acc = None
            for t, k_off, n_rows, row0 in key_tiles:
                blk, _ = tiles[t]
                p = p_bufs[slot][j, row0:row0 + n_rows, :]
                keys = slice(k_off, k_off + n_rows)
                if ka == kb:
                    part = jnp.dot(vt_ref[blk, ka * HEAD_DIM:(ka + 1) * HEAD_DIM, keys], p,
                                   preferred_element_type=f32)
                else:
                    part = jnp.concatenate(
                        [jnp.dot(vt_ref[blk, ka * HEAD_DIM:(ka + 1) * HEAD_DIM, keys], p[:, :tb],
                                 preferred_element_type=f32),
                         jnp.dot(vt_ref[blk, kb * HEAD_DIM:(kb + 1) * HEAD_DIM, keys], p[:, tb:],
                                 preferred_element_type=f32)], axis=1)
                acc = part if acc is None else acc + part
                yield
            ot = acc * (1.0 / d_bufs[slot][j, 0:1, :])
            pair = jnp.concatenate([ot[:, :tb], ot[:, tb:]], axis=0)
            o_ref[pl.ds(q0, tb), j * LANES:(j + 1) * LANES] = pair.T.astype(o_ref.dtype)
            if emit_lse:
                lse = l_bufs[slot][j, 0:1, :]
                both = jnp.concatenate([jnp.broadcast_to(lse[:, :tb], (HEAD_DIM, tb)),
                                        jnp.broadcast_to(lse[:, tb:], (HEAD_DIM, tb))], axis=0)
                lse_ref[pl.ds(q0, tb), j * LANES:(j + 1) * LANES] = both.T

    def do_round(r, parity, first=False, last=False):
        cur = [parity * ROUND_BLOCKS + b for b in range(ROUND_BLOCKS)]
        oth = [(1 - parity) * ROUND_BLOCKS + b for b in range(ROUND_BLOCKS)]
        for j in range(n_pairs):
            for b in range(ROUND_BLOCKS):
                streams = [softmax(cur[b], [j])]
                if not first:
                    streams.append(values((r - 1) * ROUND_BLOCKS + b, oth[b], [j]))
                if not last:
                    streams.append(scores((r + 1) * ROUND_BLOCKS + b, oth[b], [j]))
                _round_robin(streams)

    all_pairs = range(n_pairs)
    for b in range(ROUND_BLOCKS):
        _round_robin([scores(b, b, all_pairs)])
    do_round(0, 0, first=True)

    def body(k, carry):
        do_round(2 * k + 1, 1)
        do_round(2 * k + 2, 0)
        return carry

    lax.fori_loop(0, (n_rounds - 2) // 2, body, 0)
    do_round(n_rounds - 1, 1, last=True)
    for b in range(ROUND_BLOCKS):
        _round_robin([values((n_rounds - 1) * ROUND_BLOCKS + b, ROUND_BLOCKS + b, all_pairs)])


def _attn(qt, k, vt, sink, *, seq, blocks_per_seq, n_q_heads, n_kv_heads, win, emit_lse, name):
    n_tok = k.shape[0]
    blocks = seq // TOKEN_BLOCK
    qw, kw = n_q_heads * HEAD_DIM, n_kv_heads * HEAD_DIM
    n_pairs = n_q_heads // 2
    key_rows = TOKEN_BLOCK if blocks_per_seq == 1 else TOKEN_BLOCK + 2 * _reach(win)
    kern = functools.partial(_attn_kernel, blocks_per_seq=blocks_per_seq, n_q_heads=n_q_heads,
                             n_kv_heads=n_kv_heads, win=win, has_sink=sink is not None,
                             emit_lse=emit_lse)
    in_specs = [pl.BlockSpec((blocks, qw, TOKEN_BLOCK), lambda b: (b, 0, 0)),
                pl.BlockSpec((seq, kw), lambda b: (b, 0)),
                pl.BlockSpec((blocks, kw, TOKEN_BLOCK), lambda b: (b, 0, 0)),
                pl.BlockSpec((_BIAS_MASKED + 1, TOKEN_BLOCK, 2 * TOKEN_BLOCK), lambda b: (0, 0, 0))]
    args = [qt, k, vt, _attn_bias(win)]
    if sink is not None:
        in_specs.append(pl.BlockSpec(memory_space=pltpu.SMEM))
        args.append(sink)
    out_specs = [pl.BlockSpec((seq, qw), lambda b: (b, 0))]
    out_shape = [jax.ShapeDtypeStruct((n_tok, qw), jnp.bfloat16)]
    if emit_lse:
        out_specs.append(pl.BlockSpec((seq, qw), lambda b: (b, 0)))
        out_shape.append(jax.ShapeDtypeStruct((n_tok, qw), jnp.float32))
    big = (n_pairs, key_rows, 2 * TOKEN_BLOCK)
    small = (n_pairs, SUBLANES, 2 * TOKEN_BLOCK)
    n_slots = 2 * ROUND_BLOCKS
    scratch_shapes = ([pltpu.VMEM(big, jnp.float32)] * n_slots
                      + [pltpu.VMEM(small, jnp.float32)] * n_slots
                      + [pltpu.VMEM(big, jnp.bfloat16)] * n_slots
                      + [pltpu.VMEM(small, jnp.float32)] * (2 * n_slots))
    return pl.pallas_call(
        kern,
        grid=(n_tok // seq,),
        in_specs=in_specs,
        out_specs=out_specs,
        out_shape=out_shape,
        scratch_shapes=scratch_shapes,
        compiler_params=pltpu.CompilerParams(
            dimension_semantics=("arbitrary",), vmem_limit_bytes=VMEM_LIMIT_BYTES),
        name=name,
    )(*args)


def _layer_norm(y, g, b):
    mu = jnp.mean(y, axis=-1, keepdims=True)
    d = y - mu
    var = jnp.mean(d * d, axis=-1, keepdims=True)
    return d * lax.rsqrt(var + LN_EPS) * g + b


def _run_halves(halves, lead):
    first, second = halves
    for _ in range(lead):
        next(first)
    _round_robin([first, second])


def _mix_kernel(*refs, alpha):
    (x_ref, oa_ref) = refs[:2]
    o_refs = refs[2:2 + N_GROUPS]
    l_refs = refs[2 + N_GROUPS:2 + 2 * N_GROUPS]
    gate_ref = refs[2 + 2 * N_GROUPS]
    wa_ref, wb_ref, wo_ref, g1_ref, b1_ref = refs[3 + 2 * N_GROUPS:8 + 2 * N_GROUPS]
    x1_ref, x1b_ref = refs[8 + 2 * N_GROUPS:10 + 2 * N_GROUPS]
    slabs = refs[10 + 2 * N_GROUPS:]
    f32, bf16 = jnp.float32, jnp.bfloat16

    for g in range(1, N_GROUPS):
        _scatter_token_order(o_refs[g], slabs[2 * g - 2], DILATIONS[g])
        _scatter_token_order(l_refs[g], slabs[2 * g - 1], DILATIONS[g])

    def half(rows):
        os_, ls_ = [o_refs[0][rows, :].astype(f32)], [l_refs[0][rows, :]]
        for g in range(1, N_GROUPS):
            os_.append(_read_slabs(slabs[2 * g - 2], rows))
            ls_.append(_read_slabs(slabs[2 * g - 1], rows))
        mx = functools.reduce(jnp.maximum, ls_)
        es = [jnp.exp(l - mx) for l in ls_]
        ob = (functools.reduce(jnp.add, [e * o for e, o in zip(es, os_)])
              / functools.reduce(jnp.add, es))
        yield
        ya = jnp.dot(oa_ref[rows, :], wa_ref[...], preferred_element_type=f32)
        yb = jnp.dot(ob.astype(bf16), wb_ref[...], preferred_element_type=f32)
        ga = gate_ref[rows, :D_MODEL].astype(f32)
        gb = gate_ref[rows, D_MODEL:].astype(f32)
        merged = (ga * ya + gb * yb).astype(bf16)
        h = jnp.dot(merged, wo_ref[...], preferred_element_type=f32)
        yield
        x1 = _layer_norm(alpha * x_ref[rows, :] + h, g1_ref[...], b1_ref[...])
        x1_ref[rows, :] = x1
        x1b_ref[rows, :] = x1.astype(bf16)
        yield

    tm = x_ref.shape[0]
    _run_halves([half(slice(0, tm // 2)), half(slice(tm // 2, tm))], lead=1)


def _mlp_kernel(*refs, alpha, emit_orders):
    (x1_ref, x1b_ref, p_ref, wup_ref, wdn_ref, wpg_ref, bpg_ref, wple_ref, g2_ref,
     b2_ref) = refs[:10]
    out_ref = refs[10]
    order_refs = refs[11:11 + N_GROUPS] if emit_orders else ()
    slab_ref = refs[-1] if emit_orders else None
    f32, bf16 = jnp.float32, jnp.bfloat16

    def half(rows):
        x1b = x1b_ref[rows, :]
        acc = alpha * x1_ref[rows, :]
        for c in range(D_FF // FF_CHUNK):
            cols = slice(c * FF_CHUNK, (c + 1) * FF_CHUNK)
            up = jnp.dot(x1b, wup_ref[:, cols], preferred_element_type=f32)
            act = jnp.square(jnp.maximum(up, 0.0)).astype(bf16)
            acc = acc + jnp.dot(act, wdn_ref[cols, :], preferred_element_type=f32)
            yield
        zg = jnp.dot(x1b, wpg_ref[...], preferred_element_type=f32) + bpg_ref[...]
        gate = 1.0 / (1.0 + jnp.exp(-zg))
        ple = gate * jnp.dot(p_ref[rows, :].astype(bf16), wple_ref[...],
                             preferred_element_type=f32)
        x2 = _layer_norm(acc + ple, g2_ref[...], b2_ref[...])
        out_ref[rows, :] = x2
        if emit_orders:
            order_refs[0][rows, :] = x2.astype(bf16)
            _fill_slabs(slab_ref, rows, x2)
        yield

    tm = x1_ref.shape[0]
    _run_halves([half(slice(0, tm // 2)), half(slice(tm // 2, tm))], lead=2)
    if emit_orders:
        _slabs_to_class_major(order_refs[1:], slab_ref)


def _post(x2d, oa, o_parts, lse_parts, gates, p_all, layer, weights, alpha, bsz, seq, emit_orders):
    tm = POST_ROWS
    n_tok = x2d.shape[0]
    per_seq = seq // tm
    const = lambda i: (0, 0)
    rows = lambda w: pl.BlockSpec((tm, w), lambda i: (i, 0))
    resident = lambda shape: pl.BlockSpec(shape, const, pipeline_mode=pl.Buffered(1))
    vec = pl.BlockSpec((1, D_MODEL), const)
    params = pltpu.CompilerParams(
        dimension_semantics=("arbitrary",), vmem_limit_bytes=VMEM_LIMIT_BYTES)

    def grouped(g):
        r = DILATIONS[g]
        if r == 1:
            return rows(B_GROUP_W)
        return pl.BlockSpec((None, r, tm // r, B_GROUP_W),
                            lambda i: (i // per_seq, 0, i % per_seq, 0))

    def as_grouped(a, g):
        r = DILATIONS[g]
        return a if r == 1 else a.reshape(bsz, r, seq // r, B_GROUP_W)

    (wa, wb, wo, g1, b1, wup, wdn, wpg, bpg, wple, g2, b2) = weights
    x1, x1b = pl.pallas_call(
        functools.partial(_mix_kernel, alpha=alpha),
        grid=(n_tok // tm,),
        in_specs=[rows(D_MODEL), rows(A_Q_W)]
        + [grouped(g) for g in range(N_GROUPS)] * 2
        + [rows(GATE_W),
           resident((A_Q_W, D_MODEL)), resident((B_GROUP_W, D_MODEL)),
           resident((D_MODEL, D_MODEL)), vec, vec],
        out_specs=[rows(D_MODEL), rows(D_MODEL)],
        out_shape=[jax.ShapeDtypeStruct((n_tok, D_MODEL), jnp.float32),
                   jax.ShapeDtypeStruct((n_tok, D_MODEL), jnp.bfloat16)],
        scratch_shapes=[_slab_scratch(tm, B_GROUP_W)] * (2 * (N_GROUPS - 1)),
        compiler_params=params,
        name="mix",
    )(x2d, oa, *[as_grouped(o, g) for g, o in enumerate(o_parts)],
      *[as_grouped(l, g) for g, l in enumerate(lse_parts)], gates, wa, wb, wo, g1, b1)

    out_specs = [rows(D_MODEL)]
    out_shape = [jax.ShapeDtypeStruct((n_tok, D_MODEL), jnp.float32)]
    scratch_shapes = []
    if emit_orders:
        specs, shapes = _x_order_specs(bsz, seq, tm)
        out_specs += specs
        out_shape += shapes
        scratch_shapes = [_slab_scratch(tm, D_MODEL)]
    return pl.pallas_call(
        functools.partial(_mlp_kernel, alpha=alpha, emit_orders=emit_orders),
        grid=(n_tok // tm,),
        in_specs=[rows(D_MODEL), rows(D_MODEL),
                  pl.BlockSpec((None, tm, PLE_DIM), lambda i: (layer, i, 0)),
                  resident((D_MODEL, D_FF)), resident((D_FF, D_MODEL)),
                  resident((D_MODEL, D_MODEL)), vec,
                  resident((PLE_DIM, D_MODEL)), vec, vec],
        out_specs=out_specs,
        out_shape=out_shape,
        scratch_shapes=scratch_shapes,
        compiler_params=params,
        name="mlp",
    )(x1, x1b, p_all, wup, wdn, wpg, bpg, wple, g2, b2)


def kernel(x, p, w_in, b_gate, a_sink, w_branch_a, w_branch_b, w_out, ln1_g, ln1_b, w_up, w_down,
           w_ple_gate, b_ple_gate, w_ple, ln2_g, ln2_b):
    bsz, seq, d_model = x.shape
    depth = w_in.shape[0]
    assert d_model == D_MODEL and w_in.shape[2] == NAT_W + T_W
    assert seq % (TOKEN_BLOCK * max(DILATIONS)) == 0 and seq % IN_PROJ_ROWS == 0
    assert POST_ROWS % (2 * SUBLANES * max(DILATIONS)) == 0
    n_tok = bsz * seq
    alpha = (2 * depth) ** 0.25
    bf16 = jnp.bfloat16
    tables = _rope_tables(seq)
    p_all = p.reshape(depth, n_tok, PLE_DIM)
    x2d = x.reshape(n_tok, d_model)
    x_orders = _reorder(x2d, bsz, seq)
    for i in range(depth):
        w_nat, w_t = _in_proj_weights(w_in[i])
        flat = [xo.reshape(n_tok, d_model) for xo in x_orders]
        outs = _in_proj(flat, w_nat, w_t, b_gate[i][None, :], tables, seq)
        ka, qat, vat = outs[:3]
        gates = outs[-1]
        (oa,) = _attn(qat, ka, vat, a_sink[i], seq=seq, blocks_per_seq=seq // TOKEN_BLOCK,
                      n_q_heads=A_Q_HEADS, n_kv_heads=A_KV_HEADS, win=A_WINDOW, emit_lse=False,
                      name="attn_a")
        o_parts, lse_parts = [], []
        for g, (window, r) in enumerate(B_GROUPS):
            kg, qtg, vtg = outs[3 + 3 * g:6 + 3 * g]
            o, lse = _attn(qtg, kg, vtg, None, seq=seq,
                           blocks_per_seq=seq // r // TOKEN_BLOCK,
                           n_q_heads=B_HEADS_PER_GROUP, n_kv_heads=B_HEADS_PER_GROUP,
                           win=(window // 2) // r, emit_lse=True, name=f"attn_b{g}")
            o_parts.append(o)
            lse_parts.append(lse)
        weights = (w_branch_a[i].astype(bf16), w_branch_b[i].astype(bf16), w_out[i].astype(bf16),
                   ln1_g[i][None, :], ln1_b[i][None, :], w_up[i].astype(bf16),
                   w_down[i].astype(bf16), w_ple_gate[i].astype(bf16), b_ple_gate[i][None, :],
                   w_ple[i].astype(bf16), ln2_g[i][None, :], ln2_b[i][None, :])
        res = _post(x2d, oa, o_parts, lse_parts, gates, p_all, i, weights, alpha, bsz, seq,
                    emit_orders=i + 1 < depth)
        x2d, x_orders = res[0], res[1:]
    return x2d.reshape(bsz, seq, d_model)
```

```python
import functools
import math

import jax
import jax.numpy as jnp
import numpy as np
from jax import lax
from jax.experimental import pallas as pl
from jax.experimental.pallas import tpu as pltpu

D_MODEL = 1024
HEAD_DIM = 64
ROT_DIM = HEAD_DIM // 4
ROT_HALF = ROT_DIM // 2
ROPE_THETA = 500000.0

A_Q_HEADS = 8
A_KV_HEADS = 2
A_WINDOW = 128
B_GROUPS = ((128, 1), (512, 4), (2048, 16))
N_GROUPS = len(B_GROUPS)
B_HEADS_PER_GROUP = 4
DILATIONS = tuple(r for _, r in B_GROUPS)

A_Q_W = A_Q_HEADS * HEAD_DIM
A_KV_W = A_KV_HEADS * HEAD_DIM
B_GROUP_W = B_HEADS_PER_GROUP * HEAD_DIM
B_W = N_GROUPS * B_GROUP_W
GATE_W = 2 * D_MODEL
D_FF = 4 * D_MODEL
PLE_DIM = 256
LN_EPS = 1e-5

LANES = 128
SUBLANES = 8
TOKEN_BLOCK = 128
VMEM_LIMIT_BYTES = 56 * 1024 * 1024
NEG_BIG = -1e30
LOG2_E = math.log2(math.e)
LN_2 = math.log(2.0)

IN_PROJ_ROWS = 1024
POST_ROWS = 512
FF_CHUNK = 1024

_NAT_KA = 0
_NAT_KB = A_KV_W
_NAT_GATE = A_KV_W + B_W
NAT_W = _NAT_GATE + GATE_W
_T_QA = 0
_T_VA = A_Q_W
_T_B = A_Q_W + A_KV_W
T_W = _T_B + 2 * B_W


def _rope_tables(seq):
    pos = jnp.arange(seq, dtype=jnp.float32)
    inv_freq = ROPE_THETA ** (-jnp.arange(0, ROT_DIM, 2, dtype=jnp.float32) / ROT_DIM)
    ang = pos[:, None] * inv_freq[None, :]
    cos, sin = jnp.cos(ang), jnp.sin(ang)
    zeros = jnp.zeros((seq, HEAD_DIM - ROT_DIM), jnp.float32)
    zeros_h = jnp.zeros((seq, ROT_HALF), jnp.float32)
    reps = LANES // HEAD_DIM
    c = jnp.tile(jnp.concatenate([cos, cos, jnp.ones_like(zeros)], axis=-1), (1, reps))
    sa = jnp.tile(jnp.concatenate([-sin, zeros_h, zeros], axis=-1), (1, reps))
    sb = jnp.tile(jnp.concatenate([zeros_h, sin, zeros], axis=-1), (1, reps))

    def orders(t):
        return jnp.stack([t.reshape(seq // r, r, -1).transpose(1, 0, 2).reshape(seq, -1)
                          for r in DILATIONS])

    c, sa, sb, cos_o, sin_o = (orders(t) for t in (c, sa, sb, cos, sin))
    return c, sa, sb, cos_o.transpose(0, 2, 1), sin_o.transpose(0, 2, 1)


def _in_proj_weights(w):
    qa, ka, va, qb, kb, vb, gl = jnp.split(
        w, (A_Q_W, A_Q_W + A_KV_W, A_Q_W + 2 * A_KV_W, A_Q_W + 2 * A_KV_W + B_W,
            A_Q_W + 2 * A_KV_W + 2 * B_W, A_Q_W + 2 * A_KV_W + 3 * B_W), axis=1)
    scale = LOG2_E / math.sqrt(HEAD_DIM)
    w_nat = jnp.concatenate([ka, kb, gl], axis=1).astype(jnp.bfloat16)
    cols = [qa * scale, va]
    for g in range(N_GROUPS):
        cols += [qb[:, g * B_GROUP_W:(g + 1) * B_GROUP_W] * scale,
                 vb[:, g * B_GROUP_W:(g + 1) * B_GROUP_W]]
    w_t = jnp.concatenate(cols, axis=1).T.astype(jnp.bfloat16)
    return w_nat, w_t


def _fill_slabs(slab_ref, rows, val):
    for s in range(slab_ref.shape[0]):
        slab_ref[s, rows, :] = val[:, s * LANES:(s + 1) * LANES]


def _read_slabs(slab_ref, rows):
    return jnp.concatenate([slab_ref[s, rows, :] for s in range(slab_ref.shape[0])], axis=1)


def _slabs_to_class_major(dst_refs, slab_ref):
    n_slabs, rows, _ = slab_ref.shape
    for r, dst_ref in zip(DILATIONS[1:], dst_refs):
        n = rows // r
        for c in range(r):
            dst_ref[c] = jnp.concatenate(
                [slab_ref[s, pl.ds(c, n, stride=r), :] for s in range(n_slabs)],
                axis=1).astype(dst_ref.dtype)


def _scatter_token_order(src_ref, slab_ref, r):
    n = src_ref.shape[1]
    for c in range(r):
        blk = src_ref[c].astype(jnp.float32)
        for s in range(slab_ref.shape[0]):
            slab_ref[s, pl.ds(c, n, stride=r), :] = blk[:, s * LANES:(s + 1) * LANES]


def _reorder_kernel(x_ref, xb_ref, *rest):
    xp_refs, slab_ref = rest[:-1], rest[-1]
    x = x_ref[...]
    xb_ref[...] = x.astype(xb_ref.dtype)
    _fill_slabs(slab_ref, slice(None), x)
    _slabs_to_class_major(xp_refs, slab_ref)


def _x_order_specs(bsz, seq, tm):
    per_seq = seq // tm
    specs = [pl.BlockSpec((tm, D_MODEL), lambda i: (i, 0))]
    shapes = [jax.ShapeDtypeStruct((bsz * seq, D_MODEL), jnp.bfloat16)]
    for r in DILATIONS[1:]:
        specs.append(pl.BlockSpec((None, r, tm // r, D_MODEL),
                                  lambda i: (i // per_seq, 0, i % per_seq, 0)))
        shapes.append(jax.ShapeDtypeStruct((bsz, r, seq // r, D_MODEL), jnp.bfloat16))
    return specs, shapes


def _slab_scratch(rows, width):
    return pltpu.VMEM((width // LANES, rows, LANES), jnp.float32)


def _reorder(x2d, bsz, seq):
    tm = POST_ROWS
    specs, shapes = _x_order_specs(bsz, seq, tm)
    return pl.pallas_call(
        _reorder_kernel,
        grid=(bsz * seq // tm,),
        in_specs=[pl.BlockSpec((tm, D_MODEL), lambda i: (i, 0))],
        out_specs=specs,
        out_shape=shapes,
        scratch_shapes=[_slab_scratch(tm, D_MODEL)],
        compiler_params=pltpu.CompilerParams(
            dimension_semantics=("arbitrary",), vmem_limit_bytes=VMEM_LIMIT_BYTES),
        name="reorder",
    )(x2d)


def _rope_lanes(t, c, sa, sb):
    up = pltpu.roll(t, LANES - ROT_HALF, axis=1)
    down = pltpu.roll(t, ROT_HALF, axis=1)
    return t * c + up * sa + down * sb


def _rope_sublanes(t, cos, sin, n_heads):
    parts = []
    for h in range(n_heads):
        a = t[h * HEAD_DIM:h * HEAD_DIM + ROT_HALF]
        b = t[h * HEAD_DIM + ROT_HALF:h * HEAD_DIM + ROT_DIM]
        parts += [a * cos - b * sin, b * cos + a * sin, t[h * HEAD_DIM + ROT_DIM:(h + 1) * HEAD_DIM]]
    return jnp.concatenate(parts, axis=0)


def _store_token_blocks(dst_ref, val):
    for j in range(val.shape[1] // LANES):
        dst_ref[j] = val[:, j * LANES:(j + 1) * LANES].astype(dst_ref.dtype)


def _in_proj_kernel(*refs):
    x_refs = refs[:N_GROUPS]
    wn_ref, wt_ref, bg_ref, cl_ref, sal_ref, sbl_ref, ct_ref, st_ref = refs[N_GROUPS:N_GROUPS + 8]
    outs = refs[N_GROUPS + 8:]
    ka_ref, qat_ref, vat_ref = outs[:3]
    b_refs = [outs[3 + 3 * g:6 + 3 * g] for g in range(N_GROUPS)]
    gate_ref = outs[3 + 3 * N_GROUPS]
    f32 = jnp.float32
    nt = (((1,), (1,)), ((), ()))

    def k_rope(acc, order):
        c, sa, sb = cl_ref[order], sal_ref[order], sbl_ref[order]
        return jnp.concatenate(
            [_rope_lanes(acc[:, t * LANES:(t + 1) * LANES], c, sa, sb)
             for t in range(acc.shape[1] // LANES)], axis=1)

    x0 = x_refs[0][...]
    acc = jnp.dot(x0, wn_ref[:, _NAT_KA:_NAT_KA + A_KV_W], preferred_element_type=f32)
    ka_ref[...] = k_rope(acc, 0).astype(ka_ref.dtype)
    acc = lax.dot_general(wt_ref[_T_QA:_T_QA + A_Q_W, :], x0, nt, preferred_element_type=f32)
    _store_token_blocks(qat_ref, _rope_sublanes(acc, ct_ref[0], st_ref[0], A_Q_HEADS))
    acc = lax.dot_general(wt_ref[_T_VA:_T_VA + A_KV_W, :], x0, nt, preferred_element_type=f32)
    _store_token_blocks(vat_ref, acc)
    for g in range(N_GROUPS):
        xg = x_refs[g][...]
        k_ref, qt_ref, vt_ref = b_refs[g]
        col = _NAT_KB + g * B_GROUP_W
        acc = jnp.dot(xg, wn_ref[:, col:col + B_GROUP_W], preferred_element_type=f32)
        k_ref[...] = k_rope(acc, g).astype(k_ref.dtype)
        row = _T_B + 2 * g * B_GROUP_W
        acc = lax.dot_general(wt_ref[row:row + B_GROUP_W, :], xg, nt, preferred_element_type=f32)
        _store_token_blocks(qt_ref, _rope_sublanes(acc, ct_ref[g], st_ref[g], B_HEADS_PER_GROUP))
        acc = lax.dot_general(wt_ref[row + B_GROUP_W:row + 2 * B_GROUP_W, :], xg, nt,
                              preferred_element_type=f32)
        _store_token_blocks(vt_ref, acc)
    chunk = 512
    for j in range(GATE_W // chunk):
        z = jnp.dot(x0, wn_ref[:, _NAT_GATE + j * chunk:_NAT_GATE + (j + 1) * chunk],
                    preferred_element_type=f32)
        z = z + bg_ref[:, j * chunk:(j + 1) * chunk]
        gate_ref[:, j * chunk:(j + 1) * chunk] = (1.0 / (1.0 + jnp.exp(-z))).astype(gate_ref.dtype)


def _in_proj(x_orders, w_nat, w_t, b_gate, tables, seq):
    tm = IN_PROJ_ROWS
    n_tok = x_orders[0].shape[0]
    per_seq = seq // tm
    blocks = tm // TOKEN_BLOCK
    const = lambda i: (0, 0)
    rows = lambda w: pl.BlockSpec((tm, w), lambda i: (i, 0))
    tblocks = lambda r: pl.BlockSpec((blocks, r, TOKEN_BLOCK), lambda i: (i, 0, 0))
    lane_tab = pl.BlockSpec((N_GROUPS, tm, LANES), lambda i: (0, i % per_seq, 0))
    sub_tab = pl.BlockSpec((N_GROUPS, ROT_HALF, tm), lambda i: (0, 0, i % per_seq))
    bf16 = jnp.bfloat16
    nat = lambda w: jax.ShapeDtypeStruct((n_tok, w), bf16)
    tsp = lambda r: jax.ShapeDtypeStruct((n_tok // TOKEN_BLOCK, r, TOKEN_BLOCK), bf16)
    out_specs = [rows(A_KV_W), tblocks(A_Q_W), tblocks(A_KV_W)]
    out_shape = [nat(A_KV_W), tsp(A_Q_W), tsp(A_KV_W)]
    for _ in range(N_GROUPS):
        out_specs += [rows(B_GROUP_W), tblocks(B_GROUP_W), tblocks(B_GROUP_W)]
        out_shape += [nat(B_GROUP_W), tsp(B_GROUP_W), tsp(B_GROUP_W)]
    out_specs.append(rows(GATE_W))
    out_shape.append(nat(GATE_W))
    return pl.pallas_call(
        _in_proj_kernel,
        grid=(n_tok // tm,),
        in_specs=[rows(D_MODEL)] * N_GROUPS + [
            pl.BlockSpec((D_MODEL, NAT_W), const, pipeline_mode=pl.Buffered(1)),
            pl.BlockSpec((T_W, D_MODEL), const, pipeline_mode=pl.Buffered(1)),
            pl.BlockSpec((1, GATE_W), const),
            lane_tab, lane_tab, lane_tab, sub_tab, sub_tab,
        ],
        out_specs=out_specs,
        out_shape=out_shape,
        compiler_params=pltpu.CompilerParams(
            dimension_semantics=("arbitrary",), vmem_limit_bytes=VMEM_LIMIT_BYTES),
        name="in_proj",
    )(*x_orders, w_nat, w_t, b_gate, *tables)


_BIAS_MASKED = 3
ROUND_BLOCKS = 4


def _attn_bias(win):
    row = np.arange(TOKEN_BLOCK)[:, None]
    col = np.arange(2 * TOKEN_BLOCK)[None, :] % TOKEN_BLOCK
    tiles = [np.where(np.abs(col - row - t * TOKEN_BLOCK) <= win, 0.0, NEG_BIG)
             for t in (-1, 0, 1)]
    tiles.append(np.full((TOKEN_BLOCK, 2 * TOKEN_BLOCK), NEG_BIG))
    return jnp.asarray(np.stack(tiles), jnp.float32)


def _reach(win):
    pack = 2 * SUBLANES
    return min(TOKEN_BLOCK, -(-win // pack) * pack)


def _round_robin(streams):
    streams = list(streams)
    while streams:
        for s in list(streams):
            if next(s, StopIteration) is StopIteration:
                streams.remove(s)


def _fold_rows(op, tile, acc):
    for r in range(tile.shape[0] // SUBLANES):
        part = tile[r * SUBLANES:(r + 1) * SUBLANES, :]
        acc = part if acc is None else op(acc, part)
    return acc


def _attn_kernel(*refs, blocks_per_seq, n_q_heads, n_kv_heads, win, has_sink, emit_lse):
    refs = list(refs)
    qt_ref, k_ref, vt_ref, bias_ref = refs[:4]
    pos = 4
    sink_ref = None
    if has_sink:
        sink_ref = refs[pos]
        pos += 1
    o_ref = refs[pos]
    pos += 1
    lse_ref = None
    if emit_lse:
        lse_ref = refs[pos]
        pos += 1
    n_slots = 2 * ROUND_BLOCKS
    s_bufs, m_bufs, p_bufs, d_bufs, l_bufs = (
        refs[pos + n_slots * n:pos + n_slots * (n + 1)] for n in range(5))

    n_blocks = qt_ref.shape[0]
    n_rounds = n_blocks // ROUND_BLOCKS
    assert n_blocks % ROUND_BLOCKS == 0 and n_rounds % 2 == 0 and n_rounds >= 4
    grp = n_q_heads // n_kv_heads
    n_pairs = n_q_heads // 2
    tb = TOKEN_BLOCK
    offsets = (0,) if blocks_per_seq == 1 else (-1, 0, 1)
    reach = _reach(win)
    key_tiles, row0 = [], 0
    for t in offsets:
        n_rows = tb if t == 0 else reach
        key_tiles.append((t, tb - reach if t < 0 else 0, n_rows, row0))
        row0 += n_rows
    f32, bf16 = jnp.float32, jnp.bfloat16
    lane = lax.broadcasted_iota(jnp.int32, (1, 2 * tb), 1)
    zeros = jnp.zeros((HEAD_DIM, tb), bf16)

    def heads_of(j):
        ha, hb = 2 * j, 2 * j + 1
        ka, kb = ha // grp, hb // grp
        assert ka // 2 == kb // 2
        return ha, hb, ka, kb

    def sink_of(j):
        ha, hb, _, _ = heads_of(j)
        return jnp.where(lane < tb, sink_ref[ha], sink_ref[hb]) * LOG2_E

    def tiles_of(i):
        local = i % blocks_per_seq
        out = {}
        for t in offsets:
            if t == 0:
                out[t] = (i, 1)
            else:
                ok = jnp.logical_and(local + t >= 0, local + t < blocks_per_seq)
                out[t] = (jnp.where(ok, i + t, i), jnp.where(ok, t + 1, _BIAS_MASKED))
        return out

    def scores(i, slot, pairs):
        tiles = tiles_of(i)
        for j in pairs:
            ha, hb, ka, kb = heads_of(j)
            qa = qt_ref[i, ha * HEAD_DIM:(ha + 1) * HEAD_DIM, :]
            qb = qt_ref[i, hb * HEAD_DIM:(hb + 1) * HEAD_DIM, :]
            halves = [jnp.concatenate([qa if ka % 2 == u else zeros,
                                       qb if kb % 2 == u else zeros], axis=1) for u in (0, 1)]
            rhs = jnp.concatenate(halves, axis=0)
            top = None
            for t, k_off, n_rows, row0 in key_tiles:
                blk, bias_idx = tiles[t]
                kt = k_ref[pl.ds(pl.multiple_of(blk * tb + k_off, 2 * SUBLANES), n_rows),
                           (ka // 2) * LANES:(ka // 2 + 1) * LANES]
                st = jnp.dot(kt, rhs, preferred_element_type=f32)
                if not (t == 0 and win >= tb - 1):
                    st = st + bias_ref[bias_idx, k_off:k_off + n_rows, :]
                s_bufs[slot][j, row0:row0 + n_rows, :] = st
                top = _fold_rows(jnp.maximum, st, top)
                yield
            m = jnp.max(top, axis=0, keepdims=True)
            if has_sink:
                m = jnp.maximum(m, sink_of(j))
            m_bufs[slot][j] = jnp.broadcast_to(m, (SUBLANES, 2 * tb))

    def softmax(slot, pairs):
        for j in pairs:
            m = m_bufs[slot][j, 0:1, :]
            tot = None
            for _, _, n_rows, row0 in key_tiles:
                e = jnp.exp2(s_bufs[slot][j, row0:row0 + n_rows, :] - m)
                p_bufs[slot][j, row0:row0 + n_rows, :] = e.astype(bf16)
                tot = _fold_rows(jnp.add, e, tot)
                yield
            denom = jnp.sum(tot, axis=0, keepdims=True)
            if has_sink:
                denom = denom + jnp.exp2(sink_of(j) - m)
            d_bufs[slot][j] = jnp.broadcast_to(denom, (SUBLANES, 2 * tb))
            if emit_lse:
                lse = (m + jnp.log2(denom)) * LN_2
                l_bufs[slot][j] = jnp.broadcast_to(lse, (SUBLANES, 2 * tb))

    def values(i, slot, pairs):
        tiles = tiles_of(i)
        q0 = pl.multiple_of(i * tb, tb)
        for j in pairs:
            _, _, ka, kb = heads_of(j)
            acc = None
            for t, k_off, n_rows, row0 in key_tiles:
                blk, _ = tiles[t]
                p = p_bufs[slot][j, row0:row0 + n_rows, :]
                keys = slice(k_off, k_off + n_rows)
                if ka == kb:
                    part = jnp.dot(vt_ref[blk, ka * HEAD_DIM:(ka + 1) * HEAD_DIM, keys], p,
                                   preferred_element_type=f32)
                else:
                    part = jnp.concatenate(
                        [jnp.dot(vt_ref[blk, ka * HEAD_DIM:(ka + 1) * HEAD_DIM, keys], p[:, :tb],
                                 preferred_element_type=f32),
                         jnp.dot(vt_ref[blk, kb * HEAD_DIM:(kb + 1) * HEAD_DIM, keys], p[:, tb:],
                                 preferred_element_type=f32)], axis=1)
                acc = part if acc is None else acc + part
                yield
            ot = acc * (1.0 / d_bufs[slot][j, 0:1, :])
            pair = jnp.concatenate([ot[:, :tb], ot[:, tb:]], axis=0)
            o_ref[pl.ds(q0, tb), j * LANES:(j + 1) * LANES] = pair.T.astype(o_ref.dtype)
            if emit_lse:
                lse = l_bufs[slot][j, 0:1, :]
                both = jnp.concatenate([jnp.broadcast_to(lse[:, :tb], (HEAD_DIM, tb)),
                                        jnp.broadcast_to(lse[:, tb:], (HEAD_DIM, tb))], axis=0)
                lse_ref[pl.ds(q0, tb), j * LANES:(j + 1) * LANES] = both.T

    def do_round(r, parity, first=False, last=False):
        cur = [parity * ROUND_BLOCKS + b for b in range(ROUND_BLOCKS)]
        oth = [(1 - parity) * ROUND_BLOCKS + b for b in range(ROUND_BLOCKS)]
        for j in range(n_pairs):
            for b in range(ROUND_BLOCKS):
                streams = [softmax(cur[b], [j])]
                if not first:
                    streams.append(values((r - 1) * ROUND_BLOCKS + b, oth[b], [j]))
                if not last:
                    streams.append(scores((r + 1) * ROUND_BLOCKS + b, oth[b], [j]))
                _round_robin(streams)

    all_pairs = range(n_pairs)
    for b in range(ROUND_BLOCKS):
        _round_robin([scores(b, b, all_pairs)])
    do_round(0, 0, first=True)

    def body(k, carry):
        do_round(2 * k + 1, 1)
        do_round(2 * k + 2, 0)
        return carry

    lax.fori_loop(0, (n_rounds - 2) // 2, body, 0)
    do_round(n_rounds - 1, 1, last=True)
    for b in range(ROUND_BLOCKS):
        _round_robin([values((n_rounds - 1) * ROUND_BLOCKS + b, ROUND_BLOCKS + b, all_pairs)])


def _attn(qt, k, vt, sink, *, seq, blocks_per_seq, n_q_heads, n_kv_heads, win, emit_lse, name):
    n_tok = k.shape[0]
    blocks = seq // TOKEN_BLOCK
    qw, kw = n_q_heads * HEAD_DIM, n_kv_heads * HEAD_DIM
    n_pairs = n_q_heads // 2
    key_rows = TOKEN_BLOCK if blocks_per_seq == 1 else TOKEN_BLOCK + 2 * _reach(win)
    kern = functools.partial(_attn_kernel, blocks_per_seq=blocks_per_seq, n_q_heads=n_q_heads,
                             n_kv_heads=n_kv_heads, win=win, has_sink=sink is not None,
                             emit_lse=emit_lse)
    in_specs = [pl.BlockSpec((blocks, qw, TOKEN_BLOCK), lambda b: (b, 0, 0)),
                pl.BlockSpec((seq, kw), lambda b: (b, 0)),
                pl.BlockSpec((blocks, kw, TOKEN_BLOCK), lambda b: (b, 0, 0)),
                pl.BlockSpec((_BIAS_MASKED + 1, TOKEN_BLOCK, 2 * TOKEN_BLOCK), lambda b: (0, 0, 0))]
    args = [qt, k, vt, _attn_bias(win)]
    if sink is not None:
        in_specs.append(pl.BlockSpec(memory_space=pltpu.SMEM))
        args.append(sink)
    out_specs = [pl.BlockSpec((seq, qw), lambda b: (b, 0))]
    out_shape = [jax.ShapeDtypeStruct((n_tok, qw), jnp.bfloat16)]
    if emit_lse:
        out_specs.append(pl.BlockSpec((seq, qw), lambda b: (b, 0)))
        out_shape.append(jax.ShapeDtypeStruct((n_tok, qw), jnp.float32))
    big = (n_pairs, key_rows, 2 * TOKEN_BLOCK)
    small = (n_pairs, SUBLANES, 2 * TOKEN_BLOCK)
    n_slots = 2 * ROUND_BLOCKS
    scratch_shapes = ([pltpu.VMEM(big, jnp.float32)] * n_slots
                      + [pltpu.VMEM(small, jnp.float32)] * n_slots
                      + [pltpu.VMEM(big, jnp.bfloat16)] * n_slots
                      + [pltpu.VMEM(small, jnp.float32)] * (2 * n_slots))
    return pl.pallas_call(
        kern,
        grid=(n_tok // seq,),
        in_specs=in_specs,
        out_specs=out_specs,
        out_shape=out_shape,
        scratch_shapes=scratch_shapes,
        compiler_params=pltpu.CompilerParams(
            dimension_semantics=("arbitrary",), vmem_limit_bytes=VMEM_LIMIT_BYTES),
        name=name,
    )(*args)


def _layer_norm(y, g, b):
    mu = jnp.mean(y, axis=-1, keepdims=True)
    d = y - mu
    var = jnp.mean(d * d, axis=-1, keepdims=True)
    return d * lax.rsqrt(var + LN_EPS) * g + b


def _run_halves(halves, lead):
    first, second = halves
    for _ in range(lead):
        next(first)
    _round_robin([first, second])


def _mix_kernel(*refs, alpha):
    (x_ref, oa_ref) = refs[:2]
    o_refs = refs[2:2 + N_GROUPS]
    l_refs = refs[2 + N_GROUPS:2 + 2 * N_GROUPS]
    gate_ref = refs[2 + 2 * N_GROUPS]
    wa_ref, wb_ref, wo_ref, g1_ref, b1_ref = refs[3 + 2 * N_GROUPS:8 + 2 * N_GROUPS]
    x1_ref = refs[8 + 2 * N_GROUPS]
    slabs = refs[9 + 2 * N_GROUPS:]
    f32, bf16 = jnp.float32, jnp.bfloat16

    for g in range(1, N_GROUPS):
        _scatter_token_order(o_refs[g], slabs[2 * g - 2], DILATIONS[g])
        _scatter_token_order(l_refs[g], slabs[2 * g - 1], DILATIONS[g])

    def half(rows):
        os_, ls_ = [o_refs[0][rows, :].astype(f32)], [l_refs[0][rows, :]]
        for g in range(1, N_GROUPS):
            os_.append(_read_slabs(slabs[2 * g - 2], rows))
            ls_.append(_read_slabs(slabs[2 * g - 1], rows))
        mx = functools.reduce(jnp.maximum, ls_)
        es = [jnp.exp(l - mx) for l in ls_]
        ob = (functools.reduce(jnp.add, [e * o for e, o in zip(es, os_)])
              / functools.reduce(jnp.add, es))
        yield
        ya = jnp.dot(oa_ref[rows, :], wa_ref[...], preferred_element_type=f32)
        yb = jnp.dot(ob.astype(bf16), wb_ref[...], preferred_element_type=f32)
        ga = gate_ref[rows, :D_MODEL].astype(f32)
        gb = gate_ref[rows, D_MODEL:].astype(f32)
        merged = (ga * ya + gb * yb).astype(bf16)
        h = jnp.dot(merged, wo_ref[...], preferred_element_type=f32)
        yield
        x1_ref[rows, :] = _layer_norm(alpha * x_ref[rows, :] + h, g1_ref[...], b1_ref[...])
        yield

    tm = x_ref.shape[0]
    _run_halves([half(slice(0, tm // 2)), half(slice(tm // 2, tm))], lead=1)


def _mlp_kernel(*refs, alpha, emit_orders):
    (x1_ref, p_ref, wup_ref, wdn_ref, wpg_ref, bpg_ref, wple_ref, g2_ref, b2_ref) = refs[:9]
    out_ref = refs[9]
    order_refs = refs[10:10 + N_GROUPS] if emit_orders else ()
    slab_ref = refs[-1] if emit_orders else None
    f32, bf16 = jnp.float32, jnp.bfloat16

    def half(rows):
        x1 = x1_ref[rows, :]
        x1b = x1.astype(bf16)
        acc = alpha * x1
        for c in range(D_FF // FF_CHUNK):
            cols = slice(c * FF_CHUNK, (c + 1) * FF_CHUNK)
            up = jnp.dot(x1b, wup_ref[:, cols], preferred_element_type=f32)
            act = jnp.square(jnp.maximum(up, 0.0)).astype(bf16)
            acc = acc + jnp.dot(act, wdn_ref[cols, :], preferred_element_type=f32)
            yield
        zg = jnp.dot(x1b, wpg_ref[...], preferred_element_type=f32) + bpg_ref[...]
        gate = 1.0 / (1.0 + jnp.exp(-zg))
        ple = gate * jnp.dot(p_ref[rows, :].astype(bf16), wple_ref[...],
                             preferred_element_type=f32)
        x2 = _layer_norm(acc + ple, g2_ref[...], b2_ref[...])
        out_ref[rows, :] = x2
        if emit_orders:
            order_refs[0][rows, :] = x2.astype(bf16)
            _fill_slabs(slab_ref, rows, x2)
        yield

    tm = x1_ref.shape[0]
    _run_halves([half(slice(0, tm // 2)), half(slice(tm // 2, tm))], lead=2)
    if emit_orders:
        _slabs_to_class_major(order_refs[1:], slab_ref)


def _post(x2d, oa, o_parts, lse_parts, gates, p_all, layer, weights, alpha, bsz, seq, emit_orders):
    tm = POST_ROWS
    n_tok = x2d.shape[0]
    per_seq = seq // tm
    const = lambda i: (0, 0)
    rows = lambda w: pl.BlockSpec((tm, w), lambda i: (i, 0))
    resident = lambda shape: pl.BlockSpec(shape, const, pipeline_mode=pl.Buffered(1))
    vec = pl.BlockSpec((1, D_MODEL), const)
    params = pltpu.CompilerParams(
        dimension_semantics=("arbitrary",), vmem_limit_bytes=VMEM_LIMIT_BYTES)

    def grouped(g):
        r = DILATIONS[g]
        if r == 1:
            return rows(B_GROUP_W)
        return pl.BlockSpec((None, r, tm // r, B_GROUP_W),
                            lambda i: (i // per_seq, 0, i % per_seq, 0))

    def as_grouped(a, g):
        r = DILATIONS[g]
        return a if r == 1 else a.reshape(bsz, r, seq // r, B_GROUP_W)

    (wa, wb, wo, g1, b1, wup, wdn, wpg, bpg, wple, g2, b2) = weights
    x1 = pl.pallas_call(
        functools.partial(_mix_kernel, alpha=alpha),
        grid=(n_tok // tm,),
        in_specs=[rows(D_MODEL), rows(A_Q_W)]
        + [grouped(g) for g in range(N_GROUPS)] * 2
        + [rows(GATE_W),
           resident((A_Q_W, D_MODEL)), resident((B_GROUP_W, D_MODEL)),
           resident((D_MODEL, D_MODEL)), vec, vec],
        out_specs=rows(D_MODEL),
        out_shape=jax.ShapeDtypeStruct((n_tok, D_MODEL), jnp.float32),
        scratch_shapes=[_slab_scratch(tm, B_GROUP_W)] * (2 * (N_GROUPS - 1)),
        compiler_params=params,
        name="mix",
    )(x2d, oa, *[as_grouped(o, g) for g, o in enumerate(o_parts)],
      *[as_grouped(l, g) for g, l in enumerate(lse_parts)], gates, wa, wb, wo, g1, b1)

    out_specs = [rows(D_MODEL)]
    out_shape = [jax.ShapeDtypeStruct((n_tok, D_MODEL), jnp.float32)]
    scratch_shapes = []
    if emit_orders:
        specs, shapes = _x_order_specs(bsz, seq, tm)
        out_specs += specs
        out_shape += shapes
        scratch_shapes = [_slab_scratch(tm, D_MODEL)]
    return pl.pallas_call(
        functools.partial(_mlp_kernel, alpha=alpha, emit_orders=emit_orders),
        grid=(n_tok // tm,),
        in_specs=[rows(D_MODEL),
                  pl.BlockSpec((None, tm, PLE_DIM), lambda i: (layer, i, 0)),
                  resident((D_MODEL, D_FF)), resident((D_FF, D_MODEL)),
                  resident((D_MODEL, D_MODEL)), vec,
                  resident((PLE_DIM, D_MODEL)), vec, vec],
        out_specs=out_specs,
        out_shape=out_shape,
        scratch_shapes=scratch_shapes,
        compiler_params=params,
        name="mlp",
    )(x1, p_all, wup, wdn, wpg, bpg, wple, g2, b2)


def kernel(x, p, w_in, b_gate, a_sink, w_branch_a, w_branch_b, w_out, ln1_g, ln1_b, w_up, w_down,
           w_ple_gate, b_ple_gate, w_ple, ln2_g, ln2_b):
    bsz, seq, d_model = x.shape
    depth = w_in.shape[0]
    assert d_model == D_MODEL and w_in.shape[2] == NAT_W + T_W
    assert seq % (TOKEN_BLOCK * max(DILATIONS)) == 0 and seq % IN_PROJ_ROWS == 0
    assert POST_ROWS % (2 * SUBLANES * max(DILATIONS)) == 0
    n_tok = bsz * seq
    alpha = (2 * depth) ** 0.25
    bf16 = jnp.bfloat16
    tables = _rope_tables(seq)
    p_all = p.reshape(depth, n_tok, PLE_DIM)
    x2d = x.reshape(n_tok, d_model)
    x_orders = _reorder(x2d, bsz, seq)
    for i in range(depth):
        w_nat, w_t = _in_proj_weights(w_in[i])
        flat = [xo.reshape(n_tok, d_model) for xo in x_orders]
        outs = _in_proj(flat, w_nat, w_t, b_gate[i][None, :], tables, seq)
        ka, qat, vat = outs[:3]
        gates = outs[-1]
        (oa,) = _attn(qat, ka, vat, a_sink[i], seq=seq, blocks_per_seq=seq // TOKEN_BLOCK,
                      n_q_heads=A_Q_HEADS, n_kv_heads=A_KV_HEADS, win=A_WINDOW, emit_lse=False,
                      name="attn_a")
        o_parts, lse_parts = [], []
        for g, (window, r) in enumerate(B_GROUPS):
            kg, qtg, vtg = outs[3 + 3 * g:6 + 3 * g]
            o, lse = _attn(qtg, kg, vtg, None, seq=seq,
                           blocks_per_seq=seq // r // TOKEN_BLOCK,
                           n_q_heads=B_HEADS_PER_GROUP, n_kv_heads=B_HEADS_PER_GROUP,
                           win=(window // 2) // r, emit_lse=True, name=f"attn_b{g}")
            o_parts.append(o)
            lse_parts.append(lse)
        weights = (w_branch_a[i].astype(bf16), w_branch_b[i].astype(bf16), w_out[i].astype(bf16),
                   ln1_g[i][None, :], ln1_b[i][None, :], w_up[i].astype(bf16),
                   w_down[i].astype(bf16), w_ple_gate[i].astype(bf16), b_ple_gate[i][None, :],
                   w_ple[i].astype(bf16), ln2_g[i][None, :], ln2_b[i][None, :])
        res = _post(x2d, oa, o_parts, lse_parts, gates, p_all, i, weights, alpha, bsz, seq,
                    emit_orders=i + 1 < depth)
        x2d, x_orders = res[0], res[1:]
    return x2d.reshape(bsz, seq, d_model)
```

```python
import functools
import math

import jax
import jax.numpy as jnp
import numpy as np
from jax import lax
from jax.experimental import pallas as pl
from jax.experimental.pallas import tpu as pltpu

D_MODEL = 1024
HEAD_DIM = 64
ROT_DIM = HEAD_DIM // 4
ROT_HALF = ROT_DIM // 2
ROPE_THETA = 500000.0

A_Q_HEADS = 8
A_KV_HEADS = 2
A_WINDOW = 128
B_GROUPS = ((128, 1), (512, 4), (2048, 16))
N_GROUPS = len(B_GROUPS)
B_HEADS_PER_GROUP = 4
DILATIONS = tuple(r for _, r in B_GROUPS)

A_Q_W = A_Q_HEADS * HEAD_DIM
A_KV_W = A_KV_HEADS * HEAD_DIM
B_GROUP_W = B_HEADS_PER_GROUP * HEAD_DIM
B_W = N_GROUPS * B_GROUP_W
GATE_W = 2 * D_MODEL
D_FF = 4 * D_MODEL
PLE_DIM = 256
LN_EPS = 1e-5

LANES = 128
SUBLANES = 8
TOKEN_BLOCK = 128
VMEM_LIMIT_BYTES = 56 * 1024 * 1024
NEG_BIG = -1e30
LOG2_E = math.log2(math.e)
LN_2 = math.log(2.0)

IN_PROJ_ROWS = 1024
POST_ROWS = 512
MIX_ROWS = 1024
FF_CHUNK = 1024

_NAT_KA = 0
_NAT_KB = A_KV_W
_NAT_GATE = A_KV_W + B_W
NAT_W = _NAT_GATE + GATE_W
_T_QA = 0
_T_VA = A_Q_W
_T_B = A_Q_W + A_KV_W
T_W = _T_B + 2 * B_W


def _rope_tables(seq):
    pos = jnp.arange(seq, dtype=jnp.float32)
    inv_freq = ROPE_THETA ** (-jnp.arange(0, ROT_DIM, 2, dtype=jnp.float32) / ROT_DIM)
    ang = pos[:, None] * inv_freq[None, :]
    cos, sin = jnp.cos(ang), jnp.sin(ang)
    zeros = jnp.zeros((seq, HEAD_DIM - ROT_DIM), jnp.float32)
    zeros_h = jnp.zeros((seq, ROT_HALF), jnp.float32)
    reps = LANES // HEAD_DIM
    c = jnp.tile(jnp.concatenate([cos, cos, jnp.ones_like(zeros)], axis=-1), (1, reps))
    sa = jnp.tile(jnp.concatenate([-sin, zeros_h, zeros], axis=-1), (1, reps))
    sb = jnp.tile(jnp.concatenate([zeros_h, sin, zeros], axis=-1), (1, reps))

    def orders(t):
        return jnp.stack([t.reshape(seq // r, r, -1).transpose(1, 0, 2).reshape(seq, -1)
                          for r in DILATIONS])

    c, sa, sb, cos_o, sin_o = (orders(t) for t in (c, sa, sb, cos, sin))
    return c, sa, sb, cos_o.transpose(0, 2, 1), sin_o.transpose(0, 2, 1)


def _in_proj_weights(w):
    qa, ka, va, qb, kb, vb, gl = jnp.split(
        w, (A_Q_W, A_Q_W + A_KV_W, A_Q_W + 2 * A_KV_W, A_Q_W + 2 * A_KV_W + B_W,
            A_Q_W + 2 * A_KV_W + 2 * B_W, A_Q_W + 2 * A_KV_W + 3 * B_W), axis=1)
    scale = LOG2_E / math.sqrt(HEAD_DIM)
    w_nat = jnp.concatenate([ka, kb, gl], axis=1).astype(jnp.bfloat16)
    cols = [qa * scale, va]
    for g in range(N_GROUPS):
        cols += [qb[:, g * B_GROUP_W:(g + 1) * B_GROUP_W] * scale,
                 vb[:, g * B_GROUP_W:(g + 1) * B_GROUP_W]]
    w_t = jnp.concatenate(cols, axis=1).T.astype(jnp.bfloat16)
    return w_nat, w_t


def _fill_slabs(slab_ref, rows, val):
    for s in range(slab_ref.shape[0]):
        slab_ref[s, rows, :] = val[:, s * LANES:(s + 1) * LANES]


def _read_slabs(slab_ref, rows):
    return jnp.concatenate([slab_ref[s, rows, :] for s in range(slab_ref.shape[0])], axis=1)


def _slabs_to_class_major(dst_refs, slab_ref):
    n_slabs, rows, _ = slab_ref.shape
    for r, dst_ref in zip(DILATIONS[1:], dst_refs):
        n = rows // r
        for c in range(r):
            dst_ref[c] = jnp.concatenate(
                [slab_ref[s, pl.ds(c, n, stride=r), :] for s in range(n_slabs)],
                axis=1).astype(dst_ref.dtype)


def _scatter_token_order(src_ref, slab_ref, r):
    n = src_ref.shape[1]
    for c in range(r):
        blk = src_ref[c].astype(jnp.float32)
        for s in range(slab_ref.shape[0]):
            slab_ref[s, pl.ds(c, n, stride=r), :] = blk[:, s * LANES:(s + 1) * LANES]


def _reorder_kernel(x_ref, xb_ref, *rest):
    xp_refs, slab_ref = rest[:-1], rest[-1]
    x = x_ref[...]
    xb_ref[...] = x.astype(xb_ref.dtype)
    _fill_slabs(slab_ref, slice(None), x)
    _slabs_to_class_major(xp_refs, slab_ref)


def _x_order_specs(bsz, seq, tm):
    per_seq = seq // tm
    specs = [pl.BlockSpec((tm, D_MODEL), lambda i: (i, 0))]
    shapes = [jax.ShapeDtypeStruct((bsz * seq, D_MODEL), jnp.bfloat16)]
    for r in DILATIONS[1:]:
        specs.append(pl.BlockSpec((None, r, tm // r, D_MODEL),
                                  lambda i: (i // per_seq, 0, i % per_seq, 0)))
        shapes.append(jax.ShapeDtypeStruct((bsz, r, seq // r, D_MODEL), jnp.bfloat16))
    return specs, shapes


def _slab_scratch(rows, width):
    return pltpu.VMEM((width // LANES, rows, LANES), jnp.float32)


def _reorder(x2d, bsz, seq):
    tm = MIX_ROWS
    specs, shapes = _x_order_specs(bsz, seq, tm)
    return pl.pallas_call(
        _reorder_kernel,
        grid=(bsz * seq // tm,),
        in_specs=[pl.BlockSpec((tm, D_MODEL), lambda i: (i, 0))],
        out_specs=specs,
        out_shape=shapes,
        scratch_shapes=[_slab_scratch(tm, D_MODEL)],
        compiler_params=pltpu.CompilerParams(
            dimension_semantics=("arbitrary",), vmem_limit_bytes=VMEM_LIMIT_BYTES),
        name="reorder",
    )(x2d)


def _rope_lanes(t, c, sa, sb):
    up = pltpu.roll(t, LANES - ROT_HALF, axis=1)
    down = pltpu.roll(t, ROT_HALF, axis=1)
    return t * c + up * sa + down * sb


def _rope_sublanes(t, cos, sin, n_heads):
    parts = []
    for h in range(n_heads):
        a = t[h * HEAD_DIM:h * HEAD_DIM + ROT_HALF]
        b = t[h * HEAD_DIM + ROT_HALF:h * HEAD_DIM + ROT_DIM]
        parts += [a * cos - b * sin, b * cos + a * sin, t[h * HEAD_DIM + ROT_DIM:(h + 1) * HEAD_DIM]]
    return jnp.concatenate(parts, axis=0)


def _store_token_blocks(dst_ref, val):
    for j in range(val.shape[1] // LANES):
        dst_ref[j] = val[:, j * LANES:(j + 1) * LANES].astype(dst_ref.dtype)


def _in_proj_kernel(*refs):
    x_refs = refs[:N_GROUPS]
    wn_ref, wt_ref, bg_ref, cl_ref, sal_ref, sbl_ref, ct_ref, st_ref = refs[N_GROUPS:N_GROUPS + 8]
    outs = refs[N_GROUPS + 8:]
    ka_ref, qat_ref, vat_ref = outs[:3]
    b_refs = [outs[3 + 3 * g:6 + 3 * g] for g in range(N_GROUPS)]
    gate_ref = outs[3 + 3 * N_GROUPS]
    f32 = jnp.float32
    nt = (((1,), (1,)), ((), ()))

    def k_rope(acc, order):
        c, sa, sb = cl_ref[order], sal_ref[order], sbl_ref[order]
        return jnp.concatenate(
            [_rope_lanes(acc[:, t * LANES:(t + 1) * LANES], c, sa, sb)
             for t in range(acc.shape[1] // LANES)], axis=1)

    x0 = x_refs[0][...]
    acc = jnp.dot(x0, wn_ref[:, _NAT_KA:_NAT_KA + A_KV_W], preferred_element_type=f32)
    ka_ref[...] = k_rope(acc, 0).astype(ka_ref.dtype)
    acc = lax.dot_general(wt_ref[_T_QA:_T_QA + A_Q_W, :], x0, nt, preferred_element_type=f32)
    _store_token_blocks(qat_ref, _rope_sublanes(acc, ct_ref[0], st_ref[0], A_Q_HEADS))
    acc = lax.dot_general(wt_ref[_T_VA:_T_VA + A_KV_W, :], x0, nt, preferred_element_type=f32)
    _store_token_blocks(vat_ref, acc)
    for g in range(N_GROUPS):
        xg = x_refs[g][...]
        k_ref, qt_ref, vt_ref = b_refs[g]
        col = _NAT_KB + g * B_GROUP_W
        acc = jnp.dot(xg, wn_ref[:, col:col + B_GROUP_W], preferred_element_type=f32)
        k_ref[...] = k_rope(acc, g).astype(k_ref.dtype)
        row = _T_B + 2 * g * B_GROUP_W
        acc = lax.dot_general(wt_ref[row:row + B_GROUP_W, :], xg, nt, preferred_element_type=f32)
        _store_token_blocks(qt_ref, _rope_sublanes(acc, ct_ref[g], st_ref[g], B_HEADS_PER_GROUP))
        acc = lax.dot_general(wt_ref[row + B_GROUP_W:row + 2 * B_GROUP_W, :], xg, nt,
                              preferred_element_type=f32)
        _store_token_blocks(vt_ref, acc)
    chunk = 512
    for j in range(GATE_W // chunk):
        z = jnp.dot(x0, wn_ref[:, _NAT_GATE + j * chunk:_NAT_GATE + (j + 1) * chunk],
                    preferred_element_type=f32)
        z = z + bg_ref[:, j * chunk:(j + 1) * chunk]
        gate_ref[:, j * chunk:(j + 1) * chunk] = (1.0 / (1.0 + jnp.exp(-z))).astype(gate_ref.dtype)


def _in_proj(x_orders, w_nat, w_t, b_gate, tables, seq):
    tm = IN_PROJ_ROWS
    n_tok = x_orders[0].shape[0]
    per_seq = seq // tm
    blocks = tm // TOKEN_BLOCK
    const = lambda i: (0, 0)
    rows = lambda w: pl.BlockSpec((tm, w), lambda i: (i, 0))
    tblocks = lambda r: pl.BlockSpec((blocks, r, TOKEN_BLOCK), lambda i: (i, 0, 0))
    lane_tab = pl.BlockSpec((N_GROUPS, tm, LANES), lambda i: (0, i % per_seq, 0))
    sub_tab = pl.BlockSpec((N_GROUPS, ROT_HALF, tm), lambda i: (0, 0, i % per_seq))
    bf16 = jnp.bfloat16
    nat = lambda w: jax.ShapeDtypeStruct((n_tok, w), bf16)
    tsp = lambda r: jax.ShapeDtypeStruct((n_tok // TOKEN_BLOCK, r, TOKEN_BLOCK), bf16)
    out_specs = [rows(A_KV_W), tblocks(A_Q_W), tblocks(A_KV_W)]
    out_shape = [nat(A_KV_W), tsp(A_Q_W), tsp(A_KV_W)]
    for _ in range(N_GROUPS):
        out_specs += [rows(B_GROUP_W), tblocks(B_GROUP_W), tblocks(B_GROUP_W)]
        out_shape += [nat(B_GROUP_W), tsp(B_GROUP_W), tsp(B_GROUP_W)]
    out_specs.append(rows(GATE_W))
    out_shape.append(nat(GATE_W))
    return pl.pallas_call(
        _in_proj_kernel,
        grid=(n_tok // tm,),
        in_specs=[rows(D_MODEL)] * N_GROUPS + [
            pl.BlockSpec((D_MODEL, NAT_W), const, pipeline_mode=pl.Buffered(1)),
            pl.BlockSpec((T_W, D_MODEL), const, pipeline_mode=pl.Buffered(1)),
            pl.BlockSpec((1, GATE_W), const),
            lane_tab, lane_tab, lane_tab, sub_tab, sub_tab,
        ],
        out_specs=out_specs,
        out_shape=out_shape,
        compiler_params=pltpu.CompilerParams(
            dimension_semantics=("arbitrary",), vmem_limit_bytes=VMEM_LIMIT_BYTES),
        name="in_proj",
    )(*x_orders, w_nat, w_t, b_gate, *tables)


_BIAS_MASKED = 3
ROUND_BLOCKS = 4


def _attn_bias(win):
    row = np.arange(TOKEN_BLOCK)[:, None]
    col = np.arange(2 * TOKEN_BLOCK)[None, :] % TOKEN_BLOCK
    tiles = [np.where(np.abs(col - row - t * TOKEN_BLOCK) <= win, 0.0, NEG_BIG)
             for t in (-1, 0, 1)]
    tiles.append(np.full((TOKEN_BLOCK, 2 * TOKEN_BLOCK), NEG_BIG))
    return jnp.asarray(np.stack(tiles), jnp.float32)


def _reach(win):
    pack = 2 * SUBLANES
    return min(TOKEN_BLOCK, -(-win // pack) * pack)


def _round_robin(streams):
    streams = list(streams)
    while streams:
        for s in list(streams):
            if next(s, StopIteration) is StopIteration:
                streams.remove(s)


def _fold_rows(op, tile, acc):
    for r in range(tile.shape[0] // SUBLANES):
        part = tile[r * SUBLANES:(r + 1) * SUBLANES, :]
        acc = part if acc is None else op(acc, part)
    return acc


def _attn_kernel(*refs, blocks_per_seq, n_q_heads, n_kv_heads, win, has_sink, emit_lse):
    refs = list(refs)
    qt_ref, k_ref, vt_ref, bias_ref = refs[:4]
    pos = 4
    sink_ref = None
    if has_sink:
        sink_ref = refs[pos]
        pos += 1
    o_ref = refs[pos]
    pos += 1
    lse_ref = None
    if emit_lse:
        lse_ref = refs[pos]
        pos += 1
    n_slots = 2 * ROUND_BLOCKS
    s_bufs, m_bufs, p_bufs, d_bufs, l_bufs = (
        refs[pos + n_slots * n:pos + n_slots * (n + 1)] for n in range(5))

    n_blocks = qt_ref.shape[0]
    n_rounds = n_blocks // ROUND_BLOCKS
    assert n_blocks % ROUND_BLOCKS == 0 and n_rounds % 2 == 0 and n_rounds >= 4
    grp = n_q_heads // n_kv_heads
    n_pairs = n_q_heads // 2
    tb = TOKEN_BLOCK
    offsets = (0,) if blocks_per_seq == 1 else (-1, 0, 1)
    reach = _reach(win)
    key_tiles, row0 = [], 0
    for t in offsets:
        n_rows = tb if t == 0 else reach
        key_tiles.append((t, tb - reach if t < 0 else 0, n_rows, row0))
        row0 += n_rows
    f32, bf16 = jnp.float32, jnp.bfloat16
    lane = lax.broadcasted_iota(jnp.int32, (1, 2 * tb), 1)
    zeros = jnp.zeros((HEAD_DIM, tb), bf16)

    def heads_of(j):
        ha, hb = 2 * j, 2 * j + 1
        ka, kb = ha // grp, hb // grp
        assert ka // 2 == kb // 2
        return ha, hb, ka, kb

    def sink_of(j):
        ha, hb, _, _ = heads_of(j)
        return jnp.where(lane < tb, sink_ref[ha], sink_ref[hb]) * LOG2_E

    def tiles_of(i):
        local = i % blocks_per_seq
        out = {}
        for t in offsets:
            if t == 0:
                out[t] = (i, 1)
            else:
                ok = jnp.logical_and(local + t >= 0, local + t < blocks_per_seq)
                out[t] = (jnp.where(ok, i + t, i), jnp.where(ok, t + 1, _BIAS_MASKED))
        return out

    def scores(i, slot, pairs):
        tiles = tiles_of(i)
        for j in pairs:
            ha, hb, ka, kb = heads_of(j)
            qa = qt_ref[i, ha * HEAD_DIM:(ha + 1) * HEAD_DIM, :]
            qb = qt_ref[i, hb * HEAD_DIM:(hb + 1) * HEAD_DIM, :]
            halves = [jnp.concatenate([qa if ka % 2 == u else zeros,
                                       qb if kb % 2 == u else zeros], axis=1) for u in (0, 1)]
            rhs = jnp.concatenate(halves, axis=0)
            top = None
            for t, k_off, n_rows, row0 in key_tiles:
                blk, bias_idx = tiles[t]
                kt = k_ref[pl.ds(pl.multiple_of(blk * tb + k_off, 2 * SUBLANES), n_rows),
                           (ka // 2) * LANES:(ka // 2 + 1) * LANES]
                st = jnp.dot(kt, rhs, preferred_element_type=f32)
                if not (t == 0 and win >= tb - 1):
                    st = st + bias_ref[bias_idx, k_off:k_off + n_rows, :]
                s_bufs[slot][j, row0:row0 + n_rows, :] = st
                top = _fold_rows(jnp.maximum, st, top)
                yield
            m = jnp.max(top, axis=0, keepdims=True)
            if has_sink:
                m = jnp.maximum(m, sink_of(j))
            m_bufs[slot][j] = jnp.broadcast_to(m, (SUBLANES, 2 * tb))

    def softmax(slot, pairs):
        for j in pairs:
            m = m_bufs[slot][j, 0:1, :]
            tot = None
            for _, _, n_rows, row0 in key_tiles:
                e = jnp.exp2(s_bufs[slot][j, row0:row0 + n_rows, :] - m)
                p_bufs[slot][j, row0:row0 + n_rows, :] = e.astype(bf16)
                tot = _fold_rows(jnp.add, e, tot)
                yield
            denom = jnp.sum(tot, axis=0, keepdims=True)
            if has_sink:
                denom = denom + jnp.exp2(sink_of(j) - m)
            d_bufs[slot][j] = jnp.broadcast_to(denom, (SUBLANES, 2 * tb))
            if emit_lse:
                lse = (m + jnp.log2(denom)) * LN_2
                l_bufs[slot][j] = jnp.broadcast_to(lse, (SUBLANES, 2 * tb))

    def values(i, slot, pairs):
        tiles = tiles_of(i)
        q0 = pl.multiple_of(i * tb, tb)
        for j in pairs:
            _, _, ka, kb = heads_of(j)
            acc = None
            for t, k_off, n_rows, row0 in key_tiles:
                blk, _ = tiles[t]
                p = p_bufs[slot][j, row0:row0 + n_rows, :]
                keys = slice(k_off, k_off + n_rows)
                if ka == kb:
                    part = jnp.dot(vt_ref[blk, ka * HEAD_DIM:(ka + 1) * HEAD_DIM, keys], p,
                                   preferred_element_type=f32)
                else:
                    part = jnp.concatenate(
                        [jnp.dot(vt_ref[blk, ka * HEAD_DIM:(ka + 1) * HEAD_DIM, keys], p[:, :tb],
                                 preferred_element_type=f32),
                         jnp.dot(vt_ref[blk, kb * HEAD_DIM:(kb + 1) * HEAD_DIM, keys], p[:, tb:],
                                 preferred_element_type=f32)], axis=1)
                acc = part if acc is None else acc + part
                yield
            ot = acc * (1.0 / d_bufs[slot][j, 0:1, :])
            pair = jnp.concatenate([ot[:, :tb], ot[:, tb:]], axis=0)
            o_ref[pl.ds(q0, tb), j * LANES:(j + 1) * LANES] = pair.T.astype(o_ref.dtype)
            if emit_lse:
                lse = l_bufs[slot][j, 0:1, :]
                both = jnp.concatenate([jnp.broadcast_to(lse[:, :tb], (HEAD_DIM, tb)),
                                        jnp.broadcast_to(lse[:, tb:], (HEAD_DIM, tb))], axis=0)
                lse_ref[pl.ds(q0, tb), j * LANES:(j + 1) * LANES] = both.T

    def do_round(r, parity, first=False, last=False):
        cur = [parity * ROUND_BLOCKS + b for b in range(ROUND_BLOCKS)]
        oth = [(1 - parity) * ROUND_BLOCKS + b for b in range(ROUND_BLOCKS)]
        for j in range(n_pairs):
            for b in range(ROUND_BLOCKS):
                streams = [softmax(cur[b], [j])]
                if not first:
                    streams.append(values((r - 1) * ROUND_BLOCKS + b, oth[b], [j]))
                if not last:
                    streams.append(scores((r + 1) * ROUND_BLOCKS + b, oth[b], [j]))
                _round_robin(streams)

    all_pairs = range(n_pairs)
    for b in range(ROUND_BLOCKS):
        _round_robin([scores(b, b, all_pairs)])
    do_round(0, 0, first=True)

    def body(k, carry):
        do_round(2 * k + 1, 1)
        do_round(2 * k + 2, 0)
        return carry

    lax.fori_loop(0, (n_rounds - 2) // 2, body, 0)
    do_round(n_rounds - 1, 1, last=True)
    for b in range(ROUND_BLOCKS):
        _round_robin([values((n_rounds - 1) * ROUND_BLOCKS + b, ROUND_BLOCKS + b, all_pairs)])


def _attn(qt, k, vt, sink, *, seq, blocks_per_seq, n_q_heads, n_kv_heads, win, emit_lse, name):
    n_tok = k.shape[0]
    blocks = seq // TOKEN_BLOCK
    qw, kw = n_q_heads * HEAD_DIM, n_kv_heads * HEAD_DIM
    n_pairs = n_q_heads // 2
    key_rows = TOKEN_BLOCK if blocks_per_seq == 1 else TOKEN_BLOCK + 2 * _reach(win)
    kern = functools.partial(_attn_kernel, blocks_per_seq=blocks_per_seq, n_q_heads=n_q_heads,
                             n_kv_heads=n_kv_heads, win=win, has_sink=sink is not None,
                             emit_lse=emit_lse)
    in_specs = [pl.BlockSpec((blocks, qw, TOKEN_BLOCK), lambda b: (b, 0, 0)),
                pl.BlockSpec((seq, kw), lambda b: (b, 0)),
                pl.BlockSpec((blocks, kw, TOKEN_BLOCK), lambda b: (b, 0, 0)),
                pl.BlockSpec((_BIAS_MASKED + 1, TOKEN_BLOCK, 2 * TOKEN_BLOCK), lambda b: (0, 0, 0))]
    args = [qt, k, vt, _attn_bias(win)]
    if sink is not None:
        in_specs.append(pl.BlockSpec(memory_space=pltpu.SMEM))
        args.append(sink)
    out_specs = [pl.BlockSpec((seq, qw), lambda b: (b, 0))]
    out_shape = [jax.ShapeDtypeStruct((n_tok, qw), jnp.bfloat16)]
    if emit_lse:
        out_specs.append(pl.BlockSpec((seq, qw), lambda b: (b, 0)))
        out_shape.append(jax.ShapeDtypeStruct((n_tok, qw), jnp.float32))
    big = (n_pairs, key_rows, 2 * TOKEN_BLOCK)
    small = (n_pairs, SUBLANES, 2 * TOKEN_BLOCK)
    n_slots = 2 * ROUND_BLOCKS
    scratch_shapes = ([pltpu.VMEM(big, jnp.float32)] * n_slots
                      + [pltpu.VMEM(small, jnp.float32)] * n_slots
                      + [pltpu.VMEM(big, jnp.bfloat16)] * n_slots
                      + [pltpu.VMEM(small, jnp.float32)] * (2 * n_slots))
    return pl.pallas_call(
        kern,
        grid=(n_tok // seq,),
        in_specs=in_specs,
        out_specs=out_specs,
        out_shape=out_shape,
        scratch_shapes=scratch_shapes,
        compiler_params=pltpu.CompilerParams(
            dimension_semantics=("arbitrary",), vmem_limit_bytes=VMEM_LIMIT_BYTES),
        name=name,
    )(*args)


def _layer_norm(y, g, b):
    mu = jnp.mean(y, axis=-1, keepdims=True)
    d = y - mu
    var = jnp.mean(d * d, axis=-1, keepdims=True)
    return d * lax.rsqrt(var + LN_EPS) * g + b


def _run_halves(halves, lead):
    first, second = halves
    for _ in range(lead):
        next(first)
    _round_robin([first, second])


def _mix_kernel(*refs, alpha):
    (x_ref, oa_ref) = refs[:2]
    o_refs = refs[2:2 + N_GROUPS]
    l_refs = refs[2 + N_GROUPS:2 + 2 * N_GROUPS]
    gate_ref = refs[2 + 2 * N_GROUPS]
    wa_ref, wb_ref, wo_ref, g1_ref, b1_ref = refs[3 + 2 * N_GROUPS:8 + 2 * N_GROUPS]
    x1_ref = refs[8 + 2 * N_GROUPS]
    slabs = refs[9 + 2 * N_GROUPS:]
    f32, bf16 = jnp.float32, jnp.bfloat16

    for g in range(1, N_GROUPS):
        _scatter_token_order(o_refs[g], slabs[2 * g - 2], DILATIONS[g])
        _scatter_token_order(l_refs[g], slabs[2 * g - 1], DILATIONS[g])

    def half(rows):
        os_, ls_ = [o_refs[0][rows, :].astype(f32)], [l_refs[0][rows, :]]
        for g in range(1, N_GROUPS):
            os_.append(_read_slabs(slabs[2 * g - 2], rows))
            ls_.append(_read_slabs(slabs[2 * g - 1], rows))
        mx = functools.reduce(jnp.maximum, ls_)
        es = [jnp.exp(l - mx) for l in ls_]
        ob = (functools.reduce(jnp.add, [e * o for e, o in zip(es, os_)])
              / functools.reduce(jnp.add, es))
        yield
        ya = jnp.dot(oa_ref[rows, :], wa_ref[...], preferred_element_type=f32)
        yb = jnp.dot(ob.astype(bf16), wb_ref[...], preferred_element_type=f32)
        ga = gate_ref[rows, :D_MODEL].astype(f32)
        gb = gate_ref[rows, D_MODEL:].astype(f32)
        merged = (ga * ya + gb * yb).astype(bf16)
        h = jnp.dot(merged, wo_ref[...], preferred_element_type=f32)
        yield
        x1_ref[rows, :] = _layer_norm(alpha * x_ref[rows, :] + h, g1_ref[...], b1_ref[...])
        yield

    tm = x_ref.shape[0]
    _run_halves([half(slice(0, tm // 2)), half(slice(tm // 2, tm))], lead=1)


def _mlp_kernel(*refs, alpha, emit_orders):
    (x1_ref, p_ref, wup_ref, wdn_ref, wpg_ref, bpg_ref, wple_ref, g2_ref, b2_ref) = refs[:9]
    out_ref = refs[9]
    order_refs = refs[10:10 + N_GROUPS] if emit_orders else ()
    slab_ref = refs[-1] if emit_orders else None
    f32, bf16 = jnp.float32, jnp.bfloat16

    def half(rows):
        x1 = x1_ref[rows, :]
        x1b = x1.astype(bf16)
        acc = alpha * x1
        for c in range(D_FF // FF_CHUNK):
            cols = slice(c * FF_CHUNK, (c + 1) * FF_CHUNK)
            up = jnp.dot(x1b, wup_ref[:, cols], preferred_element_type=f32)
            act = jnp.square(jnp.maximum(up, 0.0)).astype(bf16)
            acc = acc + jnp.dot(act, wdn_ref[cols, :], preferred_element_type=f32)
            yield
        zg = jnp.dot(x1b, wpg_ref[...], preferred_element_type=f32) + bpg_ref[...]
        gate = 1.0 / (1.0 + jnp.exp(-zg))
        ple = gate * jnp.dot(p_ref[rows, :].astype(bf16), wple_ref[...],
                             preferred_element_type=f32)
        x2 = _layer_norm(acc + ple, g2_ref[...], b2_ref[...])
        out_ref[rows, :] = x2
        if emit_orders:
            order_refs[0][rows, :] = x2.astype(bf16)
            _fill_slabs(slab_ref, rows, x2)
        yield

    tm = x1_ref.shape[0]
    _run_halves([half(slice(0, tm // 2)), half(slice(tm // 2, tm))], lead=2)
    if emit_orders:
        _slabs_to_class_major(order_refs[1:], slab_ref)


def _post(x2d, oa, o_parts, lse_parts, gates, p_all, layer, weights, alpha, bsz, seq, emit_orders):
    tm = MIX_ROWS
    n_tok = x2d.shape[0]
    per_seq = seq // tm
    const = lambda i: (0, 0)
    rows = lambda w: pl.BlockSpec((tm, w), lambda i: (i, 0))
    resident = lambda shape: pl.BlockSpec(shape, const, pipeline_mode=pl.Buffered(1))
    vec = pl.BlockSpec((1, D_MODEL), const)
    params = pltpu.CompilerParams(
        dimension_semantics=("arbitrary",), vmem_limit_bytes=VMEM_LIMIT_BYTES)

    def grouped(g):
        r = DILATIONS[g]
        if r == 1:
            return rows(B_GROUP_W)
        return pl.BlockSpec((None, r, tm // r, B_GROUP_W),
                            lambda i: (i // per_seq, 0, i % per_seq, 0))

    def as_grouped(a, g):
        r = DILATIONS[g]
        return a if r == 1 else a.reshape(bsz, r, seq // r, B_GROUP_W)

    (wa, wb, wo, g1, b1, wup, wdn, wpg, bpg, wple, g2, b2) = weights
    x1 = pl.pallas_call(
        functools.partial(_mix_kernel, alpha=alpha),
        grid=(n_tok // tm,),
        in_specs=[rows(D_MODEL), rows(A_Q_W)]
        + [grouped(g) for g in range(N_GROUPS)] * 2
        + [rows(GATE_W),
           resident((A_Q_W, D_MODEL)), resident((B_GROUP_W, D_MODEL)),
           resident((D_MODEL, D_MODEL)), vec, vec],
        out_specs=rows(D_MODEL),
        out_shape=jax.ShapeDtypeStruct((n_tok, D_MODEL), jnp.float32),
        scratch_shapes=[_slab_scratch(tm, B_GROUP_W)] * (2 * (N_GROUPS - 1)),
        compiler_params=params,
        name="mix",
    )(x2d, oa, *[as_grouped(o, g) for g, o in enumerate(o_parts)],
      *[as_grouped(l, g) for g, l in enumerate(lse_parts)], gates, wa, wb, wo, g1, b1)

    tm = POST_ROWS
    rows = lambda w: pl.BlockSpec((tm, w), lambda i: (i, 0))
    out_specs = [rows(D_MODEL)]
    out_shape = [jax.ShapeDtypeStruct((n_tok, D_MODEL), jnp.float32)]
    scratch_shapes = []
    if emit_orders:
        specs, shapes = _x_order_specs(bsz, seq, tm)
        out_specs += specs
        out_shape += shapes
        scratch_shapes = [_slab_scratch(tm, D_MODEL)]
    return pl.pallas_call(
        functools.partial(_mlp_kernel, alpha=alpha, emit_orders=emit_orders),
        grid=(n_tok // tm,),
        in_specs=[rows(D_MODEL),
                  pl.BlockSpec((None, tm, PLE_DIM), lambda i: (layer, i, 0)),
                  resident((D_MODEL, D_FF)), resident((D_FF, D_MODEL)),
                  resident((D_MODEL, D_MODEL)), vec,
                  resident((PLE_DIM, D_MODEL)), vec, vec],
        out_specs=out_specs,
        out_shape=out_shape,
        scratch_shapes=scratch_shapes,
        compiler_params=params,
        name="mlp",
    )(x1, p_all, wup, wdn, wpg, bpg, wple, g2, b2)


def kernel(x, p, w_in, b_gate, a_sink, w_branch_a, w_branch_b, w_out, ln1_g, ln1_b, w_up, w_down,
           w_ple_gate, b_ple_gate, w_ple, ln2_g, ln2_b):
    bsz, seq, d_model = x.shape
    depth = w_in.shape[0]
    assert d_model == D_MODEL and w_in.shape[2] == NAT_W + T_W
    assert seq % (TOKEN_BLOCK * max(DILATIONS)) == 0 and seq % IN_PROJ_ROWS == 0
    assert POST_ROWS % (2 * SUBLANES * max(DILATIONS)) == 0
    n_tok = bsz * seq
    alpha = (2 * depth) ** 0.25
    bf16 = jnp.bfloat16
    tables = _rope_tables(seq)
    p_all = p.reshape(depth, n_tok, PLE_DIM)
    x2d = x.reshape(n_tok, d_model)
    x_orders = _reorder(x2d, bsz, seq)
    for i in range(depth):
        w_nat, w_t = _in_proj_weights(w_in[i])
        flat = [xo.reshape(n_tok, d_model) for xo in x_orders]
        outs = _in_proj(flat, w_nat, w_t, b_gate[i][None, :], tables, seq)
        ka, qat, vat = outs[:3]
        gates = outs[-1]
        (oa,) = _attn(qat, ka, vat, a_sink[i], seq=seq, blocks_per_seq=seq // TOKEN_BLOCK,
                      n_q_heads=A_Q_HEADS, n_kv_heads=A_KV_HEADS, win=A_WINDOW, emit_lse=False,
                      name="attn_a")
        o_parts, lse_parts = [], []
        for g, (window, r) in enumerate(B_GROUPS):
            kg, qtg, vtg = outs[3 + 3 * g:6 + 3 * g]
            o, lse = _attn(qtg, kg, vtg, None, seq=seq,
                           blocks_per_seq=seq // r // TOKEN_BLOCK,
                           n_q_heads=B_HEADS_PER_GROUP, n_kv_heads=B_HEADS_PER_GROUP,
                           win=(window // 2) // r, emit_lse=True, name=f"attn_b{g}")
            o_parts.append(o)
            lse_parts.append(lse)
        weights = (w_branch_a[i].astype(bf16), w_branch_b[i].astype(bf16), w_out[i].astype(bf16),
                   ln1_g[i][None, :], ln1_b[i][None, :], w_up[i].astype(bf16),
                   w_down[i].astype(bf16), w_ple_gate[i].astype(bf16), b_ple_gate[i][None, :],
                   w_ple[i].astype(bf16), ln2_g[i][None, :], ln2_b[i][None, :])
        res = _post(x2d, oa, o_parts, lse_parts, gates, p_all, i, weights, alpha, bsz, seq,
                    emit_orders=i + 1 < depth)
        x2d, x_orders = res[0], res[1:]
    return x2d.reshape(bsz, seq, d_model)
```
